```python
import math
import jax, jax.numpy as jnp
from jax import lax
import numpy as np

D_MODEL = 1024
BATCH = 2
SEQ = 16384
DEPTH = 2

ATT_HEADS = 8
HEAD_DIM = 64
ATT_WIDTH = ATT_HEADS * HEAD_DIM
DILATIONS = ((128, 1), (512, 4), (2048, 16))
ATT_BLOCK = 128
ROPE_THETA = 10000.0

S5_WIDTH = 512
S5_GROUP = 16
S5_GROUPS = S5_WIDTH // S5_GROUP
S5_STATE = 64

RWKV_HEADS = 8
RWKV_HEAD = 64
RWKV_WIDTH = RWKV_HEADS * RWKV_HEAD
RWKV_DECAY_LORA = 64
RWKV_AAA_LORA = 64
RWKV_MV_LORA = 32
RWKV_GATE_LORA = 128
RWKV_GN_EPS = 64e-5

LRU_WIDTH = 512
LRU_BLOCKS = 8
LRU_BLOCK = LRU_WIDTH // LRU_BLOCKS
LRU_CONV = 4
LRU_C = 8.0

N_BRANCH = 4
D_FF = -(-8 * D_MODEL // (3 * 256)) * 256
IN_SIZES = (ATT_WIDTH, ATT_WIDTH, ATT_WIDTH, S5_WIDTH, RWKV_WIDTH, RWKV_WIDTH, RWKV_WIDTH, LRU_WIDTH, N_BRANCH * D_MODEL)
N_IN = 3 * ATT_WIDTH + S5_WIDTH + 3 * RWKV_WIDTH + LRU_WIDTH + N_BRANCH * D_MODEL
RMS_EPS = 1e-6

kernel_name = "hybrid_gated_parallel_mixers"


def rms_norm(x, g):
    xf = x.astype(jnp.float32)
    y = xf * lax.rsqrt(jnp.mean(xf * xf, axis=-1, keepdims=True) + RMS_EPS)
    return (y * g.astype(jnp.float32)).astype(x.dtype)


def rope_tables(seq, dim):
    inv = 1.0 / (ROPE_THETA ** (jnp.arange(0, dim, 2, dtype=jnp.float32) / dim))
    ang = jnp.arange(seq, dtype=jnp.float32)[:, None] * inv[None, :]
    return jnp.cos(ang), jnp.sin(ang)


def apply_rope(t, cos, sin):
    tf = t.astype(jnp.float32)
    t1, t2 = jnp.split(tf, 2, axis=-1)
    c, s = cos[None, :, None, :], sin[None, :, None, :]
    return jnp.concatenate([t1 * c - t2 * s, t2 * c + t1 * s], axis=-1)


def causal_shift(t):
    return jnp.pad(t, ((0, 0), (1, 0), (0, 0)))[:, :-1]


def banded_causal_attention(q, k, v, n_back):
    n, length, heads, dh = q.shape
    blk = ATT_BLOCK
    nb = -(-length // blk)
    pad = ((0, 0), (0, nb * blk - length), (0, 0), (0, 0))
    qb = jnp.pad(q, pad).reshape(n, nb, blk, heads, dh)
    kb = jnp.pad(k, pad).reshape(n, nb, blk, heads, dh)
    vb = jnp.pad(v, pad).reshape(n, nb, blk, heads, dh)

    def with_prev(t):
        prev = jnp.concatenate([jnp.zeros_like(t[:, :1]), t[:, :-1]], axis=1)
        return jnp.concatenate([prev, t], axis=2)

    kc, vc = with_prev(kb), with_prev(vb)
    sc = jnp.einsum('nbqhd,nbkhd->nbhqk', qb, kc) * (dh ** -0.5)
    kj = jnp.arange(2 * blk)[None, :]
    dist = jnp.arange(blk)[:, None] + blk - kj
    band = (dist >= 0) & (dist <= n_back)
    after_start = (jnp.arange(nb)[:, None] > 0) | (kj >= blk)
    mask = band[None, :, :] & after_start[:, None, :]
    sc = jnp.where(mask[None, :, None], sc, -jnp.inf)
    m = jnp.max(sc, axis=-1, keepdims=True)
    e = jnp.exp(sc - m)
    den = jnp.sum(e, axis=-1, keepdims=True)
    o = jnp.einsum('nbhqk,nbkhd->nbqhd', e / den, vc).reshape(n, nb * blk, heads, dh)[:, :length]
    lse = (m + jnp.log(den))[..., 0]
    lse = jnp.transpose(lse, (0, 1, 3, 2)).reshape(n, nb * blk, heads)[:, :length]
    return o, lse


def dilated_window_attention(q, k, v):
    b, s, h, dh = q.shape
    outs, lses = [], []
    for window, dil in DILATIONS:
        sub = s // dil

        def to_sub(t):
            return t.reshape(b, sub, dil, h, dh).transpose(0, 2, 1, 3, 4).reshape(b * dil, sub, h, dh)

        o, lse = banded_causal_attention(to_sub(q), to_sub(k), to_sub(v), window // dil)
        outs.append(o.reshape(b, dil, sub, h, dh).transpose(0, 2, 1, 3, 4).reshape(b, s, h, dh))
        lses.append(lse.reshape(b, dil, sub, h).transpose(0, 2, 1, 3).reshape(b, s, h))
    wts = jax.nn.softmax(jnp.stack(lses, axis=0), axis=0)
    return jnp.einsum('gbsh,gbshd->bshd', wts, jnp.stack(outs, axis=0)).reshape(b, s, h * dh)


def complex_linear_combine(e1, e2):
    a1r, a1i, b1r, b1i = e1
    a2r, a2i, b2r, b2i = e2
    return (a2r * a1r - a2i * a1i,
            a2r * a1i + a2i * a1r,
            a2r * b1r - a2i * b1i + b2r,
            a2r * b1i + a2i * b1r + b2i)


def s5_mixer(u, lam_re, lam_im, log_dt, b_re, b_im, c_re, c_im, d_skip, w_glu, b_glu):
    f32 = jnp.float32
    bsz, s, _ = u.shape
    uf = u.astype(f32)
    ug = uf.reshape(bsz, s, S5_GROUPS, S5_GROUP)
    lr, li = lam_re.astype(f32), lam_im.astype(f32)
    dt = jnp.exp(log_dt.astype(f32))[:, None]
    mag = jnp.exp(lr * dt)
    ab_re, ab_im = mag * jnp.cos(li * dt), mag * jnp.sin(li * dt)
    nr = ab_re - 1.0
    den = lr * lr + li * li
    f_re = (nr * lr + ab_im * li) / den
    f_im = (ab_im * lr - nr * li) / den
    bb_re = f_re[..., None] * b_re - f_im[..., None] * b_im
    bb_im = f_re[..., None] * b_im + f_im[..., None] * b_re
    bu_re = jnp.einsum('bsgc,gpc->bsgp', ug, bb_re)
    bu_im = jnp.einsum('bsgc,gpc->bsgp', ug, bb_im)
    a_re = jnp.broadcast_to(ab_re, (1, s) + ab_re.shape)
    a_im = jnp.broadcast_to(ab_im, (1, s) + ab_im.shape)
    _, _, x_re, x_im = lax.associative_scan(complex_linear_combine, (a_re, a_im, bu_re, bu_im), axis=1)
    y = jnp.einsum('bsgp,gcp->bsgc', x_re, c_re) - jnp.einsum('bsgp,gcp->bsgc', x_im, c_im)
    y = jax.nn.gelu(y.reshape(bsz, s, S5_WIDTH) + d_skip * uf)
    val, gate = jnp.split(y @ w_glu + b_glu, 2, axis=-1)
    return val * jax.nn.sigmoid(gate)


def wkv7_scan(r, w, k, v, a, b):
    bsz, _, h, n = r.shape

    def step(state, inp):
        r_t, w_t, k_t, v_t, a_t, b_t = inp
        sa = jnp.einsum('bhij,bhj->bhi', state, a_t)
        state = state * w_t[:, :, None, :] + sa[..., None] * b_t[:, :, None, :] + v_t[..., None] * k_t[:, :, None, :]
        return state, jnp.einsum('bhij,bhj->bhi', state, r_t)

    xs = tuple(jnp.moveaxis(t, 1, 0) for t in (r, w, k, v, a, b))
    _, ys = lax.scan(step, jnp.zeros((bsz, h, n, n), jnp.float32), xs)
    return jnp.moveaxis(ys, 0, 1)


def rwkv7_time_mix(h, r_p, k_p, v_p, mu_rkv, mu_wag, w0, w1, w2, a0, a1, a2, g1, g2,
                   k_k, k_a, r_k, gn_w, gn_b, v_res):
    f32 = jnp.float32
    bsz, s, _ = h.shape
    hf = h.astype(f32)
    dh = causal_shift(hf) - hf
    xw, xa, xg = (hf + dh * mu_wag[i] for i in range(3))

    def lerp(p, mu):
        p = p.astype(f32)
        return p + (causal_shift(p) - p) * mu

    r = lerp(r_p, mu_rkv[0])
    k = lerp(k_p, mu_rkv[1])
    v = lerp(v_p, mu_rkv[2])
    w_log = -jax.nn.softplus(-(w0 + jnp.tanh(xw @ w1) @ w2)) - 0.5
    decay = jnp.exp(-jnp.exp(w_log))
    if v_res is not None:
        v_first, mu_v, v0, v1, v2 = v_res
        xv = hf + dh * mu_v
        v = v + (v_first - v) * jax.nn.sigmoid(v0 + (xv @ v1) @ v2)
    a = jax.nn.sigmoid(a0 + (xa @ a1) @ a2)
    g = jax.nn.sigmoid(xg @ g1) @ g2

    def heads(t):
        return t.reshape(bsz, s, RWKV_HEADS, RWKV_HEAD)

    kk = heads(k * k_k)
    kk = kk * lax.rsqrt(jnp.maximum(jnp.sum(kk * kk, axis=-1, keepdims=True), 1e-24))
    k = k * (1.0 + (a - 1.0) * k_a)
    rh, kh, vh = heads(r), heads(k), heads(v)
    y = wkv7_scan(rh, heads(decay), kh, vh, -kk, kk * heads(a))
    mean = jnp.mean(y, axis=-1, keepdims=True)
    var = jnp.mean(jnp.square(y - mean), axis=-1, keepdims=True)
    y = ((y - mean) * lax.rsqrt(var + RWKV_GN_EPS)).reshape(bsz, s, RWKV_WIDTH) * gn_w + gn_b
    bonus = jnp.sum(rh * kh * r_k, axis=-1, keepdims=True) * vh
    y = y + bonus.reshape(bsz, s, RWKV_WIDTH)
    return y * g, v


def linear_combine(e1, e2):
    a1, b1 = e1
    a2, b2 = e2
    return a1 * a2, a2 * b1 + b2


def rglru_mixer(x_in, conv_w, conv_b, w_a, b_a, w_x, b_x, lam):
    f32 = jnp.float32
    bsz, s, c = x_in.shape
    xc = lax.conv_general_dilated(x_in.astype(f32), conv_w.astype(f32)[:, None, :], window_strides=(1,),
                                  padding=((LRU_CONV - 1, 0),), dimension_numbers=('NWC', 'WIO', 'NWC'),
                                  feature_group_count=c) + conv_b
    xb = xc.reshape(bsz, s, LRU_BLOCKS, LRU_BLOCK)
    r = jax.nn.sigmoid(jnp.einsum('bshi,hij->bshj', xb, w_a).reshape(bsz, s, c) + b_a)
    i = jax.nn.sigmoid(jnp.einsum('bshi,hij->bshj', xb, w_x).reshape(bsz, s, c) + b_x)
    log_a = -LRU_C * r * jax.nn.softplus(-lam.astype(f32))
    a = jnp.exp(log_a)
    inp = jnp.sqrt(-jnp.expm1(2.0 * log_a)) * (i * xc)
    _, hs = lax.associative_scan(linear_combine, (a, inp), axis=1)
    return hs


def setup_inputs(seed: int = 0) -> dict:
    key = jax.random.key(seed)
    ks = iter(jax.random.split(key, 64))
    f32 = jnp.float32
    L, D, F = DEPTH, D_MODEL, D_FF
    G, P = S5_GROUPS, S5_STATE

    def nrm(shape, scale):
        return jax.random.normal(next(ks), shape, f32) * scale

    def unif(shape, lo, hi):
        return jax.random.uniform(next(ks), shape, f32, lo, hi)

    lru_u = unif((L, LRU_WIDTH), 0.9, 0.999)
    lru_a0 = lru_u ** (1.0 / LRU_C)
    return {
        "x": nrm((BATCH, SEQ, D), 1.0),
        "norm_mix": 1.0 + nrm((L, D), 0.02),
        "w_in": nrm((L, D, N_IN), D ** -0.5),
        "s5_lam_re": -0.5 * jnp.exp(nrm((L, G, P), 0.02)),
        "s5_lam_im": math.pi * jnp.arange(P, dtype=f32) + nrm((L, G, P), 0.01),
        "s5_log_dt": unif((L, G), math.log(1e-3), math.log(1e-1)),
        "s5_b_re": nrm((L, G, P, S5_GROUP), (2 * S5_GROUP) ** -0.5),
        "s5_b_im": nrm((L, G, P, S5_GROUP), (2 * S5_GROUP) ** -0.5),
        "s5_c_re": nrm((L, G, S5_GROUP, P), 0.5),
        "s5_c_im": nrm((L, G, S5_GROUP, P), 0.5),
        "s5_d": nrm((L, S5_WIDTH), 1.0),
        "s5_w_glu": nrm((L, S5_WIDTH, 2 * S5_WIDTH), S5_WIDTH ** -0.5),
        "s5_b_glu": nrm((L, 2 * S5_WIDTH), 0.02),
        "rw_mu_rkv": unif((L, 3, RWKV_WIDTH), 0.0, 1.0),
        "rw_mu_wag": unif((L, 3, D), 0.0, 1.0),
        "rw_w0": unif((L, RWKV_WIDTH), -6.0, -1.0),
        "rw_w1": nrm((L, D, RWKV_DECAY_LORA), D ** -0.5),
        "rw_w2": nrm((L, RWKV_DECAY_LORA, RWKV_WIDTH), 0.1 * RWKV_DECAY_LORA ** -0.5),
        "rw_a0": nrm((L, RWKV_WIDTH), 0.1),
        "rw_a1": nrm((L, D, RWKV_AAA_LORA), D ** -0.5),
        "rw_a2": nrm((L, RWKV_AAA_LORA, RWKV_WIDTH), 0.5 * RWKV_AAA_LORA ** -0.5),
        "rw_g1": nrm((L, D, RWKV_GATE_LORA), D ** -0.5),
        "rw_g2": nrm((L, RWKV_GATE_LORA, RWKV_WIDTH), RWKV_GATE_LORA ** -0.5),
        "rw_k_k": 0.85 + nrm((L, RWKV_WIDTH), 0.02),
        "rw_k_a": 1.0 + nrm((L, RWKV_WIDTH), 0.02),
        "rw_r_k": nrm((L, RWKV_HEADS, RWKV_HEAD), 0.1),
        "rw_gn_w": 1.0 + nrm((L, RWKV_WIDTH), 0.02),
        "rw_gn_b": nrm((L, RWKV_WIDTH), 0.02),
        "rw_mu_v": unif((L - 1, D), 0.0, 1.0),
        "rw_v0": 0.5 + nrm((L - 1, RWKV_WIDTH), 0.1),
        "rw_v1": nrm((L - 1, D, RWKV_MV_LORA), D ** -0.5),
        "rw_v2": nrm((L - 1, RWKV_MV_LORA, RWKV_WIDTH), 0.5 * RWKV_MV_LORA ** -0.5),
        "lru_conv_w": nrm((L, LRU_CONV, LRU_WIDTH), LRU_CONV ** -0.5),
        "lru_conv_b": nrm((L, LRU_WIDTH), 0.02),
        "lru_w_a": nrm((L, LRU_BLOCKS, LRU_BLOCK, LRU_BLOCK), LRU_BLOCK ** -0.5),
        "lru_b_a": nrm((L, LRU_WIDTH), 0.02),
        "lru_w_x": nrm((L, LRU_BLOCKS, LRU_BLOCK, LRU_BLOCK), LRU_BLOCK ** -0.5),
        "lru_b_x": nrm((L, LRU_WIDTH), 0.02),
        "lru_lam": jnp.log(lru_a0) - jnp.log1p(-lru_a0),
        "w_branch": nrm((L, N_BRANCH, ATT_WIDTH, D), ATT_WIDTH ** -0.5),
        "w_out": nrm((L, D, D), D ** -0.5),
        "norm_ffn": 1.0 + nrm((L, D), 0.02),
        "w_ffn_gate": nrm((L, D, F), D ** -0.5),
        "w_ffn_up": nrm((L, D, F), D ** -0.5),
        "w_ffn_down": nrm((L, F, D), F ** -0.5),
        "norm_final": 1.0 + nrm((D,), 0.02),
    }


def reference(x, norm_mix, w_in,
              s5_lam_re, s5_lam_im, s5_log_dt, s5_b_re, s5_b_im, s5_c_re, s5_c_im, s5_d, s5_w_glu, s5_b_glu,
              rw_mu_rkv, rw_mu_wag, rw_w0, rw_w1, rw_w2, rw_a0, rw_a1, rw_a2, rw_g1, rw_g2,
              rw_k_k, rw_k_a, rw_r_k, rw_gn_w, rw_gn_b, rw_mu_v, rw_v0, rw_v1, rw_v2,
              lru_conv_w, lru_conv_b, lru_w_a, lru_b_a, lru_w_x, lru_b_x, lru_lam,
              w_branch, w_out, norm_ffn, w_ffn_gate, w_ffn_up, w_ffn_down, norm_final):
    f32 = jnp.float32
    b, s, _ = x.shape
    cos, sin = rope_tables(s, HEAD_DIM)
    splits = [int(c) for c in np.cumsum(IN_SIZES)[:-1]]

    def att_heads(t):
        return t.reshape(b, s, ATT_HEADS, HEAD_DIM)

    v_first = None
    for l in range(DEPTH):
        h = rms_norm(x, norm_mix[l])
        proj = h @ w_in[l]
        q, k, v, u, r_p, k_p, v_p, x_lru, gate_logits = jnp.split(proj, splits, axis=-1)

        y_a = dilated_window_attention(apply_rope(att_heads(q), cos, sin),
                                       apply_rope(att_heads(k), cos, sin),
                                       att_heads(v).astype(f32))
        y_b = s5_mixer(u, s5_lam_re[l], s5_lam_im[l], s5_log_dt[l], s5_b_re[l], s5_b_im[l],
                       s5_c_re[l], s5_c_im[l], s5_d[l], s5_w_glu[l], s5_b_glu[l])
        v_res = None if l == 0 else (v_first, rw_mu_v[l - 1], rw_v0[l - 1], rw_v1[l - 1], rw_v2[l - 1])
        y_c, v_c = rwkv7_time_mix(h, r_p, k_p, v_p, rw_mu_rkv[l], rw_mu_wag[l], rw_w0[l], rw_w1[l], rw_w2[l],
                                  rw_a0[l], rw_a1[l], rw_a2[l], rw_g1[l], rw_g2[l], rw_k_k[l], rw_k_a[l],
                                  rw_r_k[l], rw_gn_w[l], rw_gn_b[l], v_res)
        if l == 0:
            v_first = v_c
        y_d = rglru_mixer(x_lru, lru_conv_w[l], lru_conv_b[l], lru_w_a[l], lru_b_a[l],
                          lru_w_x[l], lru_b_x[l], lru_lam[l])

        gates = jax.nn.sigmoid(gate_logits.astype(f32)).reshape(b, s, N_BRANCH, D_MODEL)
        merged = gates[:, :, 0] * (y_a @ w_branch[l, 0])
        for n, y_n in enumerate((y_b, y_c, y_d), start=1):
            merged = merged + gates[:, :, n] * (y_n @ w_branch[l, n])
        x = x + (merged @ w_out[l]).astype(x.dtype)

        h2 = rms_norm(x, norm_ffn[l])
        ff = (jax.nn.silu(h2 @ w_ffn_gate[l]) * (h2 @ w_ffn_up[l])) @ w_ffn_down[l]
        x = x + ff.astype(x.dtype)
    return rms_norm(x, norm_final)
```

```python
import functools
import math

import jax
import jax.numpy as jnp
import numpy as np
from jax import lax
from jax.experimental import pallas as pl
from jax.experimental.pallas import tpu as pltpu

F32 = jnp.float32
BF16 = jnp.bfloat16

D_MODEL = 1024
N_HEADS = 8
HEAD_DIM = 64
WIDTH = N_HEADS * HEAD_DIM
DILATIONS = (1, 4, 16)
ATT_BLOCK = 128
ROPE_THETA = 10000.0
S5_GROUPS = 32
S5_GROUP = 16
S5_STATE = 64
S5_NSTATE = S5_GROUPS * S5_STATE
RWKV_GN_EPS = 64e-5
RWKV_CHUNK = 64
LORA_HALF = 384
LRU_BLOCKS = 8
LRU_C = 8.0
D_FF = 2816
RMS_EPS = 1e-6
NEG_BIG = -1e30

LANES = 128
SUBLANES = 8
VMEM_LIMIT = 56 * 1024 * 1024

NT_DIMS = (((1,), (1,)), ((), ()))


def _dot(a, b):
    return jnp.dot(a, b, preferred_element_type=F32)


def _dot_nt(a, b):
    return lax.dot_general(a, b, NT_DIMS, preferred_element_type=F32)


def _split2(x):
    hi = x.astype(BF16)
    lo = (x - hi.astype(F32)).astype(BF16)
    return hi, lo


def _split3(x):
    hi = x.astype(BF16)
    r1 = x - hi.astype(F32)
    mid = r1.astype(BF16)
    lo = (r1 - mid.astype(F32)).astype(BF16)
    return hi, mid, lo


def _dot_sel(x, sel):
    hi, lo = _split2(x)
    return _dot(hi, sel) + _dot(lo, sel)


def _dot3(a, b):
    ah, al = _split2(a)
    bh, bl = _split2(b)
    return _dot(ah, bh) + _dot(ah, bl) + _dot(al, bh)


def _sigmoid(x):
    return 1.0 / (1.0 + jnp.exp(-x))


def _softplus(z):
    return jnp.maximum(z, 0.0) + jnp.log1p(jnp.exp(-jnp.abs(z)))


def _rms(x, g):
    ms = jnp.mean(x * x, axis=-1, keepdims=True)
    return x * lax.rsqrt(ms + RMS_EPS) * g


def _const_spec(shape):
    n = len(shape)
    return pl.BlockSpec(shape, lambda *_: (0,) * n)


def _params(sem):
    return pltpu.CompilerParams(dimension_semantics=sem, vmem_limit_bytes=VMEM_LIMIT)


def _inproj_body(x_ref, g_ref, cos_ref, sin_ref, wqk_ref, wv_ref, wu_ref, wrkv_ref, wlru_ref, wlora_ref,
                 h_ref, q_ref, k_ref, v_ref, u_ref, rkv_ref, lru_ref, lora_ref):
    hb = _rms(x_ref[...], g_ref[...]).astype(BF16)
    h_ref[...] = hb
    qk = _dot(hb, wqk_ref[...])
    reps = 2 * WIDTH // LANES
    cos = jnp.concatenate([cos_ref[...]] * reps, axis=-1)
    sin = jnp.concatenate([sin_ref[...]] * reps, axis=-1)
    lane = lax.broadcasted_iota(jnp.int32, qk.shape, 1)
    first_half = (lane % HEAD_DIM) < (HEAD_DIM // 2)
    partner = jnp.where(first_half,
                        pltpu.roll(qk, 2 * WIDTH - HEAD_DIM // 2, 1),
                        pltpu.roll(qk, HEAD_DIM // 2, 1))
    rot = qk * cos + partner * sin
    q_ref[...] = (rot[:, :WIDTH] * (HEAD_DIM ** -0.5)).astype(BF16)
    k_ref[...] = rot[:, WIDTH:].astype(BF16)
    v_ref[...] = _dot(hb, wv_ref[...]).astype(BF16)
    u_ref[...] = _dot(hb, wu_ref[...])
    rkv_ref[...] = _dot(hb, wrkv_ref[...])
    lru_ref[...] = _dot(hb, wlru_ref[...])
    lora_ref[...] = _dot(hb, wlora_ref[...])


def _inproj(x, g, cos, sin, wqk, wv, wu, wrkv, wlru, wlora, tm=256):
    b, s, d = x.shape
    grid = (b, s // tm)
    row = lambda w: pl.BlockSpec((None, tm, w), lambda bi, i: (bi, i, 0))
    tab = pl.BlockSpec((tm, LANES), lambda bi, i: (i, 0))
    widths = (d, WIDTH, WIDTH, WIDTH, WIDTH, 3 * WIDTH, WIDTH, 2 * LORA_HALF)
    dtypes = (BF16, BF16, BF16, BF16, F32, F32, F32, F32)
    return pl.pallas_call(
        _inproj_body,
        grid=grid,
        in_specs=[row(d), _const_spec((1, d)), tab, tab] + [_const_spec(w.shape) for w in (wqk, wv, wu, wrkv, wlru, wlora)],
        out_specs=[row(w) for w in widths],
        out_shape=[jax.ShapeDtypeStruct((b, s, w), dt) for w, dt in zip(widths, dtypes)],
        compiler_params=_params(("parallel", "parallel")),
        name="inproj",
    )(x, g, cos, sin, wqk, wv, wu, wrkv, wlru, wlora)


def _attn_body(q_ref, kc_ref, kp_ref, vc_ref, vp_ref, o_ref, l_ref):
    i = pl.program_id(2)
    blk = ATT_BLOCK
    row = lax.broadcasted_iota(jnp.int32, (blk, blk), 0)
    col = lax.broadcasted_iota(jnp.int32, (blk, blk), 1)
    mask_c = col <= row
    mask_p = jnp.logical_and(col >= row, i > 0)
    lse_tile = jnp.zeros((blk, LANES), F32)
    for h in range(N_HEADS):
        sl = slice(h * HEAD_DIM, (h + 1) * HEAD_DIM)
        q = q_ref[:, sl]
        sc = jnp.where(mask_c, _dot_nt(q, kc_ref[:, sl]), NEG_BIG)
        sp = jnp.where(mask_p, _dot_nt(q, kp_ref[:, sl]), NEG_BIG)
        m = jnp.maximum(jnp.max(sc, axis=-1, keepdims=True), jnp.max(sp, axis=-1, keepdims=True))
        pc = jnp.exp(sc - m)
        pp = jnp.exp(sp - m)
        den = jnp.sum(pc, axis=-1, keepdims=True) + jnp.sum(pp, axis=-1, keepdims=True)
        o = _dot(pc.astype(BF16), vc_ref[:, sl]) + _dot(pp.astype(BF16), vp_ref[:, sl])
        o_ref[:, sl] = o / den
        lse_tile = jnp.where(col == h, m + jnp.log(den), lse_tile)
    l_ref[...] = lse_tile


def _attention(q, k, v, dil):
    b, s, w = q.shape
    sub = s // dil
    view = lambda t: t.reshape(b, sub, dil * w)
    cur = pl.BlockSpec((None, ATT_BLOCK, w), lambda bi, r, i: (bi, i, r))
    prev = pl.BlockSpec((None, ATT_BLOCK, w), lambda bi, r, i: (bi, jnp.maximum(i - 1, 0), r))
    o, lse = pl.pallas_call(
        _attn_body,
        grid=(b, dil, sub // ATT_BLOCK),
        in_specs=[cur, cur, prev, cur, prev],
        out_specs=[cur, pl.BlockSpec((None, ATT_BLOCK, LANES), lambda bi, r, i: (bi, i, r))],
        out_shape=[jax.ShapeDtypeStruct((b, sub, dil * w), F32),
                   jax.ShapeDtypeStruct((b, sub, dil * LANES), F32)],
        compiler_params=_params(("parallel", "parallel", "parallel")),
        name=f"attn_d{dil}",
    )(view(q), view(k), view(k), view(v), view(v))
    return o.reshape(b, s, w), lse.reshape(b, s, LANES)


def _gelu_tanh(x):
    return 0.5 * x * (1.0 + jnp.tanh(math.sqrt(2.0 / math.pi) * (x + 0.044715 * (x * x * x))))


def _s5_body(u_ref, bbig_ref, are_ref, aim_ref, cbig_ref, d_ref, wglu_ref, bglu_ref, o_ref, xs_ref, st_ref):
    tb = u_ref.shape[0]
    n = S5_NSTATE

    @pl.when(pl.program_id(1) == 0)
    def _():
        st_ref[...] = jnp.zeros_like(st_ref)

    u = u_ref[...]
    xs_ref[...] = _dot(u.astype(BF16), bbig_ref[...])
    a_re = are_ref[...]
    a_im = aim_ref[...]

    def step(t, carry):
        xr, xi = carry
        bu = xs_ref[pl.ds(t, 1), :]
        nr = a_re * xr - a_im * xi + bu[:, :n]
        ni = a_re * xi + a_im * xr + bu[:, n:]
        xs_ref[pl.ds(t, 1), :] = jnp.concatenate([nr, ni], axis=-1)
        return nr, ni

    xr, xi = lax.fori_loop(0, tb, step, (st_ref[0:1, :n], st_ref[0:1, n:]), unroll=8)
    st_ref[0:1, :] = jnp.concatenate([xr, xi], axis=-1)
    y = _dot(xs_ref[...].astype(BF16), cbig_ref[...]) + d_ref[...] * u
    z = _dot(_gelu_tanh(y).astype(BF16), wglu_ref[...]) + bglu_ref[...]
    o_ref[...] = z[:, :WIDTH] * _sigmoid(z[:, WIDTH:])


def _s5(u, bbig, a_re, a_im, cbig, d_skip, wglu, bglu, tb=256):
    b, s, w = u.shape
    row = pl.BlockSpec((None, tb, w), lambda bi, i: (bi, i, 0))
    consts = (bbig, a_re, a_im, cbig, d_skip, wglu, bglu)
    return pl.pallas_call(
        _s5_body,
        grid=(b, s // tb),
        in_specs=[row] + [_const_spec(c.shape) for c in consts],
        out_specs=row,
        out_shape=jax.ShapeDtypeStruct((b, s, w), F32),
        scratch_shapes=[pltpu.VMEM((tb, 2 * S5_NSTATE), F32), pltpu.VMEM((SUBLANES, 2 * S5_NSTATE), F32)],
        compiler_params=_params(("parallel", "arbitrary")),
        name="s5",
    )(u, *consts)


V_MU_R, V_MU_K, V_MU_V, V_W0, V_A0, V_V0, V_KK, V_KA, V_RK, V_GNW, V_GNB = range(11)


def _shift_rows(x, carry_ref):
    full = jnp.concatenate([carry_ref[...], x], axis=0)
    shifted = pltpu.roll(full, 1, 0)[SUBLANES:]
    carry_ref[...] = x[x.shape[0] - SUBLANES:]
    return shifted


def _rwkv_body(has_vres, *refs):
    if has_vres:
        (rkv_ref, lora_ref, vf_ref, vec_ref, w2_ref, a2_ref, g2_ref, v2_ref, hsum_ref,
         y_ref, st_ref, carry_ref, ysc_ref) = refs
    else:
        (rkv_ref, lora_ref, vec_ref, w2_ref, a2_ref, g2_ref, hsum_ref,
         y_ref, vout_ref, st_ref, carry_ref, ysc_ref) = refs
    L = rkv_ref.shape[0]
    w = WIDTH

    @pl.when(pl.program_id(1) == 0)
    def _():
        st_ref[...] = jnp.zeros_like(st_ref)
        carry_ref[...] = jnp.zeros_like(carry_ref)

    vec = vec_ref[...]
    row = lambda j: vec[j:j + 1, :]
    hsum = hsum_ref[...]

    rkv = rkv_ref[...]
    lora = lora_ref[...]
    cat = jnp.concatenate([rkv, lora[:, LORA_HALF:]], axis=-1)
    prev = _shift_rows(cat, carry_ref)
    lerp = lambda j, mu: rkv[:, j * w:(j + 1) * w] + (prev[:, j * w:(j + 1) * w] - rkv[:, j * w:(j + 1) * w]) * mu
    r = lerp(0, row(V_MU_R))
    k = lerp(1, row(V_MU_K))
    v = lerp(2, row(V_MU_V))
    pre = lora[:, :LORA_HALF] + prev[:, 3 * w:]

    wx = row(V_W0) + _dot(jnp.tanh(pre[:, 0:64]).astype(BF16), w2_ref[...])
    logw = -jnp.exp(-_softplus(-wx) - 0.5)
    a = _sigmoid(row(V_A0) + _dot(pre[:, 64:128].astype(BF16), a2_ref[...]))
    g = _dot(_sigmoid(pre[:, 128:256]).astype(BF16), g2_ref[...])
    if has_vres:
        v = v + (vf_ref[...] - v) * _sigmoid(row(V_V0) + _dot(pre[:, 256:320].astype(BF16), v2_ref[...]))
    else:
        vout_ref[...] = v

    kk = k * row(V_KK)
    kk = kk * lax.rsqrt(jnp.maximum(_dot_sel(kk * kk, hsum), 1e-24))
    k = k * (1.0 + (a - 1.0) * row(V_KA))
    bonus = _dot_sel(r * k * row(V_RK), hsum) * v

    ti = lax.broadcasted_iota(jnp.int32, (L, L), 0)
    tj = lax.broadcasted_iota(jnp.int32, (L, L), 1)
    tri = jnp.where(tj <= ti, 1.0, 0.0).astype(BF16)
    hi, mid, lo = _split3(logw)
    cl = _dot(tri, hi) + _dot(tri, mid) + _dot(tri, lo)
    e_neg = jnp.exp(-cl)
    a_t = (-kk) * jnp.exp(cl - logw)
    b_t = ((kk * a) * e_neg).astype(BF16)
    k_t = (k * e_neg).astype(BF16)
    r_t = (r * jnp.exp(cl)).astype(BF16)
    a_tt = a_t.T.astype(BF16)
    v_tt = v.T.astype(BF16)
    a_t = a_t.astype(BF16)
    v_b = v.astype(BF16)
    p_end = jnp.exp(cl[L - 1:L, :])

    upper_strict = ti < tj
    lower_incl = tj <= ti
    eye = (ti == tj).astype(F32)
    n_double = int(math.log2(L)) - 1

    for h in range(N_HEADS):
        sl = slice(h * HEAD_DIM, (h + 1) * HEAD_DIM)
        ah, bh, kh, rh, vh = a_t[:, sl], b_t[:, sl], k_t[:, sl], r_t[:, sl], v_b[:, sl]
        tab = jnp.where(upper_strict, _dot_nt(bh, ah), 0.0)
        tak = jnp.where(upper_strict, _dot_nt(kh, ah), 0.0)
        trb = jnp.where(lower_incl, _dot_nt(rh, bh), 0.0)
        trk = jnp.where(lower_incl, _dot_nt(rh, kh), 0.0)
        inv = eye + tab
        pw = tab
        for _ in range(n_double):
            pw = _dot3(pw, pw)
            inv = inv + _dot3(inv, pw)
        st = st_ref[h]
        stb = st.astype(BF16)
        rhs = _dot(stb, a_tt[sl, :]) + _dot(v_tt[sl, :], tak.astype(BF16))
        ut = _dot3(rhs, inv)
        utb = ut.astype(BF16)
        ysc_ref[:, sl] = _dot_nt(rh, stb) + _dot_nt(trb.astype(BF16), utb) + _dot(trk.astype(BF16), vh)
        st_ref[h] = (st + _dot(utb, bh) + _dot(v_tt[sl, :], kh)) * p_end[:, sl]

    y = ysc_ref[...]
    mean = _dot_sel(y, hsum) * (1.0 / HEAD_DIM)
    dev = y - mean
    var = _dot_sel(dev * dev, hsum) * (1.0 / HEAD_DIM)
    y = dev * lax.rsqrt(var + RWKV_GN_EPS) * row(V_GNW) + row(V_GNB)
    y_ref[...] = (y + bonus) * g


def _rwkv(rkv, lora, v_first, vec, w2, a2, g2, v2, hsum):
    b, s, _ = rkv.shape
    L = RWKV_CHUNK
    has_vres = v_first is not None
    row = lambda wd: pl.BlockSpec((None, L, wd), lambda bi, i: (bi, i, 0))
    ins = [rkv, lora] + ([v_first] if has_vres else []) + [vec, w2, a2, g2] + ([v2] if has_vres else []) + [hsum]
    in_specs = [row(3 * WIDTH), row(2 * LORA_HALF)] + ([row(WIDTH)] if has_vres else [])
    in_specs += [_const_spec(c.shape) for c in ins[len(in_specs):]]
    y_shape = jax.ShapeDtypeStruct((b, s, WIDTH), F32)
    out = pl.pallas_call(
        functools.partial(_rwkv_body, has_vres),
        grid=(b, s // L),
        in_specs=in_specs,
        out_specs=row(WIDTH) if has_vres else [row(WIDTH), row(WIDTH)],
        out_shape=y_shape if has_vres else [y_shape, y_shape],
        scratch_shapes=[pltpu.VMEM((N_HEADS, HEAD_DIM, HEAD_DIM), F32),
                        pltpu.VMEM((SUBLANES, 3 * WIDTH + LORA_HALF), F32),
                        pltpu.VMEM((L, WIDTH), F32)],
        compiler_params=_params(("parallel", "arbitrary")),
        name="rwkv_l1" if has_vres else "rwkv_l0",
    )(*ins)
    return (out, v_first) if has_vres else (out[0], out[1])


def _lru_body(x_ref, cw_ref, cb_ref, wa_ref, ba_ref, wx_ref, bx_ref, lam_ref, o_ref, a_ref, carry_ref, h_ref):
    tb = x_ref.shape[0]

    @pl.when(pl.program_id(1) == 0)
    def _():
        carry_ref[...] = jnp.zeros_like(carry_ref)
        h_ref[...] = jnp.zeros_like(h_ref)

    x = x_ref[...]
    full = jnp.concatenate([carry_ref[...], x], axis=0)
    carry_ref[...] = x[tb - SUBLANES:]
    cw = cw_ref[...]
    taps = cw.shape[0]
    xc = cb_ref[...] + cw[taps - 1:taps, :] * x
    for j in range(1, taps):
        xc = xc + cw[taps - 1 - j:taps - j, :] * pltpu.roll(full, j, 0)[SUBLANES:]
    xcb = xc.astype(BF16)
    r = _sigmoid(_dot(xcb, wa_ref[...]) + ba_ref[...])
    i = _sigmoid(_dot(xcb, wx_ref[...]) + bx_ref[...])
    log_a = -LRU_C * r * _softplus(-lam_ref[...])
    a = jnp.exp(log_a)
    a_ref[...] = a
    o_ref[...] = jnp.sqrt(-jnp.tanh(log_a) * (a * a + 1.0)) * (i * xc)

    def step(t, h):
        h = a_ref[pl.ds(t, 1), :] * h + o_ref[pl.ds(t, 1), :]
        o_ref[pl.ds(t, 1), :] = h
        return h

    h_ref[0:1, :] = lax.fori_loop(0, tb, step, h_ref[0:1, :], unroll=8)


def _lru(x, cw, cb, wa, ba, wx, bx, lam, tb=512):
    b, s, w = x.shape
    row = pl.BlockSpec((None, tb, w), lambda bi, i: (bi, i, 0))
    consts = (cw, cb, wa, ba, wx, bx, lam)
    return pl.pallas_call(
        _lru_body,
        grid=(b, s // tb),
        in_specs=[row] + [_const_spec(c.shape) for c in consts],
        out_specs=row,
        out_shape=jax.ShapeDtypeStruct((b, s, w), F32),
        scratch_shapes=[pltpu.VMEM((tb, w), F32), pltpu.VMEM((SUBLANES, w), F32), pltpu.VMEM((SUBLANES, w), F32)],
        compiler_params=_params(("parallel", "arbitrary")),
        name="lru",
    )(x, *consts)


def _merge_body(x_ref, h_ref, o1_ref, o2_ref, o3_ref, l1_ref, l2_ref, l3_ref, yb_ref, yc_ref, yd_ref,
                wg_ref, wb_ref, wo_ref, exp_ref, out_ref):
    hb = h_ref[...]
    l1, l2, l3 = l1_ref[...], l2_ref[...], l3_ref[...]
    m = jnp.maximum(jnp.maximum(l1, l2), l3)
    e1, e2, e3 = jnp.exp(l1 - m), jnp.exp(l2 - m), jnp.exp(l3 - m)
    inv = 1.0 / (e1 + e2 + e3)
    expand = exp_ref[...]
    ya = (_dot_sel(e1 * inv, expand) * o1_ref[...] + _dot_sel(e2 * inv, expand) * o2_ref[...]
          + _dot_sel(e3 * inv, expand) * o3_ref[...])
    merged = None
    for n, y in enumerate((ya, yb_ref[...], yc_ref[...], yd_ref[...])):
        gate = _sigmoid(_dot(hb, wg_ref[:, n * D_MODEL:(n + 1) * D_MODEL]))
        term = gate * _dot(y.astype(BF16), wb_ref[n])
        merged = term if merged is None else merged + term
    out_ref[...] = x_ref[...] + _dot(merged.astype(BF16), wo_ref[...])


def _merge(x, h, os, ls, yb, yc, yd, wg, wb, wo, expand, tm=256):
    b, s, d = x.shape
    row = lambda wd: pl.BlockSpec((None, tm, wd), lambda bi, i: (bi, i, 0))
    consts = (wg, wb, wo, expand)
    return pl.pallas_call(
        _merge_body,
        grid=(b, s // tm),
        in_specs=[row(d), row(d)] + [row(WIDTH)] * 3 + [row(LANES)] * 3 + [row(WIDTH)] * 3
        + [_const_spec(c.shape) for c in consts],
        out_specs=row(d),
        out_shape=jax.ShapeDtypeStruct((b, s, d), F32),
        compiler_params=_params(("parallel", "parallel")),
        name="merge",
    )(x, h, *os, *ls, yb, yc, yd, *consts)


def _ffn_body(final, x_ref, g_ref, wg_ref, wu_ref, wd_ref, gf_ref, out_ref):
    x = x_ref[...]
    hb = _rms(x, g_ref[...]).astype(BF16)
    gate = _dot(hb, wg_ref[...])
    up = _dot(hb, wu_ref[...])
    y = x + _dot((gate * _sigmoid(gate) * up).astype(BF16), wd_ref[...])
    out_ref[...] = _rms(y, gf_ref[...]) if final else y


def _ffn(x, g, wg, wu, wd, gf, final, tm=256):
    b, s, d = x.shape
    row = pl.BlockSpec((None, tm, d), lambda bi, i: (bi, i, 0))
    consts = (g, wg, wu, wd, gf)
    return pl.pallas_call(
        functools.partial(_ffn_body, final),
        grid=(b, s // tm),
        in_specs=[row] + [_const_spec(c.shape) for c in consts],
        out_specs=row,
        out_shape=jax.ShapeDtypeStruct((b, s, d), F32),
        compiler_params=_params(("parallel", "parallel")),
        name="ffn_final" if final else "ffn",
    )(x, *consts)


def _rope_tables(s):
    half = HEAD_DIM // 2
    inv = 1.0 / (ROPE_THETA ** (jnp.arange(0, HEAD_DIM, 2, dtype=F32) / HEAD_DIM))
    ang = jnp.arange(s, dtype=F32)[:, None] * inv[None, :]
    cos, sin = jnp.cos(ang), jnp.sin(ang)
    reps = LANES // HEAD_DIM
    cos_t = jnp.tile(jnp.concatenate([cos, cos], axis=-1), (1, reps))
    sin_t = jnp.tile(jnp.concatenate([-sin, sin], axis=-1), (1, reps))
    return cos_t, sin_t


def _block_diag(blocks):
    g, r, c = blocks.shape
    eye = jnp.eye(g, dtype=blocks.dtype)
    return (eye[:, None, :, None] * blocks[:, :, None, :]).reshape(g * r, g * c)


def _s5_params(lam_re, lam_im, log_dt, b_re, b_im, c_re, c_im):
    lr, li = lam_re.astype(F32), lam_im.astype(F32)
    dt = jnp.exp(log_dt.astype(F32))[:, None]
    mag = jnp.exp(lr * dt)
    ab_re, ab_im = mag * jnp.cos(li * dt), mag * jnp.sin(li * dt)
    nr = ab_re - 1.0
    den = lr * lr + li * li
    f_re = (nr * lr + ab_im * li) / den
    f_im = (ab_im * lr - nr * li) / den
    bb_re = f_re[..., None] * b_re - f_im[..., None] * b_im
    bb_im = f_re[..., None] * b_im + f_im[..., None] * b_re
    to_in = lambda t: _block_diag(jnp.transpose(t, (0, 2, 1)))
    bbig = jnp.concatenate([to_in(bb_re), to_in(bb_im)], axis=-1).astype(BF16)
    to_out = lambda t: _block_diag(jnp.transpose(t, (0, 2, 1)))
    cbig = jnp.concatenate([to_out(c_re), -to_out(c_im)], axis=0).astype(BF16)
    return bbig, ab_re.reshape(1, -1), ab_im.reshape(1, -1), cbig


def _lora_weights(mu_wag, w1, a1, g1, mu_v, v1):
    d = w1.shape[0]
    pad = lambda t: jnp.pad(t, ((0, 0), (0, LORA_HALF - t.shape[1])))
    v1 = jnp.zeros((d, 32), F32) if v1 is None else v1
    mu_v = jnp.zeros((d,), F32) if mu_v is None else mu_v
    mats = (w1, a1, g1, v1)
    mus = (mu_wag[0], mu_wag[1], mu_wag[2], mu_v)
    keep = pad(jnp.concatenate([m * (1.0 - mu)[:, None] for m, mu in zip(mats, mus)], axis=1))
    shifted = pad(jnp.concatenate([m * mu[:, None] for m, mu in zip(mats, mus)], axis=1))
    return jnp.concatenate([keep, shifted], axis=1).astype(BF16)


def kernel(x, norm_mix, w_in, s5_lam_re, s5_lam_im, s5_log_dt, s5_b_re, s5_b_im, s5_c_re, s5_c_im, s5_d, s5_w_glu, s5_b_glu, rw_mu_rkv, rw_mu_wag, rw_w0, rw_w1, rw_w2, rw_a0, rw_a1, rw_a2, rw_g1, rw_g2, rw_k_k, rw_k_a, rw_r_k, rw_gn_w, rw_gn_b, rw_mu_v, rw_v0, rw_v1, rw_v2, lru_conv_w, lru_conv_b, lru_w_a, lru_b_a, lru_w_x, lru_b_x, lru_lam, w_branch, w_out, norm_ffn, w_ffn_gate, w_ffn_up, w_ffn_down, norm_final):
    b, s, d = x.shape
    depth = w_in.shape[0]
    w = WIDTH
    cos_t, sin_t = _rope_tables(s)
    head_id = jnp.arange(w) // HEAD_DIM
    hsum = (head_id[:, None] == head_id[None, :]).astype(BF16)
    expand = (jnp.arange(LANES)[:, None] == head_id[None, :]).astype(BF16)
    bf = lambda t: t.astype(BF16)
    row = lambda t: t.reshape(1, -1).astype(F32)

    v_first = None
    for l in range(depth):
        wl = w_in[l]
        cols = np.cumsum((0, w, w, w, w, w, w, w, w))
        seg = lambda i, j: bf(wl[:, cols[i]:cols[j]])
        has_vres = l > 0
        wlora = _lora_weights(rw_mu_wag[l], rw_w1[l], rw_a1[l], rw_g1[l],
                              rw_mu_v[l - 1] if has_vres else None, rw_v1[l - 1] if has_vres else None)
        h, q, k, v, u, rkv, xlru, lora = _inproj(
            x, row(norm_mix[l]), cos_t, sin_t, seg(0, 2), seg(2, 3), seg(3, 4), seg(4, 7), seg(7, 8), wlora)

        att = [_attention(q, k, v, dil) for dil in DILATIONS]

        bbig, a_re, a_im, cbig = _s5_params(s5_lam_re[l], s5_lam_im[l], s5_log_dt[l], s5_b_re[l], s5_b_im[l],
                                            s5_c_re[l], s5_c_im[l])
        y_b = _s5(u, bbig, a_re, a_im, cbig, row(s5_d[l]), bf(s5_w_glu[l]), row(s5_b_glu[l]))

        zero = jnp.zeros((w,), F32)
        vec = jnp.stack([rw_mu_rkv[l, 0], rw_mu_rkv[l, 1], rw_mu_rkv[l, 2], rw_w0[l], rw_a0[l],
                         rw_v0[l - 1] if has_vres else zero, rw_k_k[l], rw_k_a[l], rw_r_k[l].reshape(-1),
                         rw_gn_w[l], rw_gn_b[l]] + [zero] * 5).astype(F32)
        v2 = bf(jnp.pad(rw_v2[l - 1], ((0, 32), (0, 0)))) if has_vres else None
        y_c, v_c = _rwkv(rkv, lora, v_first, vec, bf(rw_w2[l]), bf(rw_a2[l]), bf(rw_g2[l]), v2, hsum)
        if l == 0:
            v_first = v_c

        y_d = _lru(xlru, lru_conv_w[l].astype(F32), row(lru_conv_b[l]), bf(_block_diag(lru_w_a[l])), row(lru_b_a[l]),
                   bf(_block_diag(lru_w_x[l])), row(lru_b_x[l]), row(lru_lam[l]))

        x = _merge(x, h, [o for o, _ in att], [ls for _, ls in att], y_b, y_c, y_d,
                   bf(wl[:, cols[8]:]), bf(w_branch[l]), bf(w_out[l]), expand)
        x = _ffn(x, row(norm_ffn[l]), bf(w_ffn_gate[l]), bf(w_ffn_up[l]), bf(w_ffn_down[l]), row(norm_final),
                 final=(l == depth - 1))
    return x
```

```python
import functools
import math

import jax
import jax.numpy as jnp
import numpy as np
from jax import lax
from jax.experimental import pallas as pl
from jax.experimental.pallas import tpu as pltpu

F32 = jnp.float32
BF16 = jnp.bfloat16

D_MODEL = 1024
N_HEADS = 8
HEAD_DIM = 64
WIDTH = N_HEADS * HEAD_DIM
DILATIONS = (1, 4, 16)
ATT_BLOCK = 128
ROPE_THETA = 10000.0
S5_GROUPS = 32
S5_GROUP = 16
S5_STATE = 64
S5_NSTATE = S5_GROUPS * S5_STATE
RWKV_GN_EPS = 64e-5
RWKV_CHUNK = 64
LORA_HALF = 384
LRU_BLOCKS = 8
LRU_C = 8.0
D_FF = 2816
RMS_EPS = 1e-6
NEG_BIG = -1e30

LANES = 128
SUBLANES = 8
S5_SEGS = SUBLANES
S5_HALVES = 2
S5_HALF_STATES = S5_NSTATE // S5_HALVES
VMEM_LIMIT = 56 * 1024 * 1024

NT_DIMS = (((1,), (1,)), ((), ()))


def _dot(a, b):
    return jnp.dot(a, b, preferred_element_type=F32)


def _dot_nt(a, b):
    return lax.dot_general(a, b, NT_DIMS, preferred_element_type=F32)


def _split2(x):
    hi = x.astype(BF16)
    lo = (x - hi.astype(F32)).astype(BF16)
    return hi, lo


def _split3(x):
    hi = x.astype(BF16)
    r1 = x - hi.astype(F32)
    mid = r1.astype(BF16)
    lo = (r1 - mid.astype(F32)).astype(BF16)
    return hi, mid, lo


def _dot_sel(x, sel):
    hi, lo = _split2(x)
    return _dot(hi, sel) + _dot(lo, sel)


def _dot3(a, b):
    ah, al = _split2(a)
    bh, bl = _split2(b)
    return _dot(ah, bh) + _dot(ah, bl) + _dot(al, bh)


def _sigmoid(x):
    return 1.0 / (1.0 + jnp.exp(-x))


def _softplus(z):
    return jnp.maximum(z, 0.0) + jnp.log1p(jnp.exp(-jnp.abs(z)))


def _rms(x, g):
    ms = jnp.mean(x * x, axis=-1, keepdims=True)
    return x * lax.rsqrt(ms + RMS_EPS) * g


def _const_spec(shape):
    n = len(shape)
    return pl.BlockSpec(shape, lambda *_: (0,) * n)


def _params(sem):
    return pltpu.CompilerParams(dimension_semantics=sem, vmem_limit_bytes=VMEM_LIMIT)


def _seg_view_spec(s, tm):
    tiles_per_seg = s // S5_SEGS // tm
    return pl.BlockSpec((None, tm, WIDTH), lambda bi, i: (bi, i % tiles_per_seg, i // tiles_per_seg))


def _inproj_body(x_ref, g_ref, cos_ref, sin_ref, wqk_ref, wv_ref, wu_ref, wrkv_ref, wlru_ref, wlora_ref,
                 h_ref, q_ref, k_ref, v_ref, u_ref, rkv_ref, lru_ref, lora_ref):
    hb = _rms(x_ref[...], g_ref[...]).astype(BF16)
    h_ref[...] = hb
    qk = _dot(hb, wqk_ref[...])
    reps = 2 * WIDTH // LANES
    cos = jnp.concatenate([cos_ref[...]] * reps, axis=-1)
    sin = jnp.concatenate([sin_ref[...]] * reps, axis=-1)
    lane = lax.broadcasted_iota(jnp.int32, qk.shape, 1)
    first_half = (lane % HEAD_DIM) < (HEAD_DIM // 2)
    partner = jnp.where(first_half,
                        pltpu.roll(qk, 2 * WIDTH - HEAD_DIM // 2, 1),
                        pltpu.roll(qk, HEAD_DIM // 2, 1))
    rot = qk * cos + partner * sin
    q_ref[...] = (rot[:, :WIDTH] * (HEAD_DIM ** -0.5)).astype(BF16)
    k_ref[...] = rot[:, WIDTH:].astype(BF16)
    v_ref[...] = _dot(hb, wv_ref[...]).astype(BF16)
    u_ref[...] = _dot(hb, wu_ref[...])
    rkv_ref[...] = _dot(hb, wrkv_ref[...])
    lru_ref[...] = _dot(hb, wlru_ref[...])
    lora_ref[...] = _dot(hb, wlora_ref[...])


def _inproj(x, g, cos, sin, wqk, wv, wu, wrkv, wlru, wlora, tm=256):
    b, s, d = x.shape
    grid = (b, s // tm)
    row = lambda w: pl.BlockSpec((None, tm, w), lambda bi, i: (bi, i, 0))
    tab = pl.BlockSpec((tm, LANES), lambda bi, i: (i, 0))
    widths = (d, WIDTH, WIDTH, WIDTH, WIDTH, 3 * WIDTH, WIDTH, 2 * LORA_HALF)
    dtypes = (BF16, BF16, BF16, BF16, F32, F32, F32, F32)
    out_specs = [row(w) for w in widths]
    out_shape = [jax.ShapeDtypeStruct((b, s, w), dt) for w, dt in zip(widths, dtypes)]
    out_specs[4] = _seg_view_spec(s, tm)
    out_shape[4] = jax.ShapeDtypeStruct((b, s // S5_SEGS, S5_SEGS * WIDTH), F32)
    return pl.pallas_call(
        _inproj_body,
        grid=grid,
        in_specs=[row(d), _const_spec((1, d)), tab, tab] + [_const_spec(w.shape) for w in (wqk, wv, wu, wrkv, wlru, wlora)],
        out_specs=out_specs,
        out_shape=out_shape,
        compiler_params=_params(("parallel", "parallel")),
        name="inproj",
    )(x, g, cos, sin, wqk, wv, wu, wrkv, wlru, wlora)


def _attn_body(q_ref, kc_ref, kp_ref, vc_ref, vp_ref, o_ref, l_ref):
    i = pl.program_id(2)
    blk = ATT_BLOCK
    row = lax.broadcasted_iota(jnp.int32, (blk, blk), 0)
    col = lax.broadcasted_iota(jnp.int32, (blk, blk), 1)
    mask_c = col <= row
    mask_p = jnp.logical_and(col >= row, i > 0)
    heads = [slice(h * HEAD_DIM, (h + 1) * HEAD_DIM) for h in range(N_HEADS)]
    qs = [q_ref[:, sl] for sl in heads]
    sc = [jnp.where(mask_c, _dot_nt(q, kc_ref[:, sl]), NEG_BIG) for q, sl in zip(qs, heads)]
    sp = [jnp.where(mask_p, _dot_nt(q, kp_ref[:, sl]), NEG_BIG) for q, sl in zip(qs, heads)]
    ms = [jnp.maximum(jnp.max(c, axis=-1, keepdims=True), jnp.max(p, axis=-1, keepdims=True))
          for c, p in zip(sc, sp)]
    pc = [jnp.exp(c - m) for c, m in zip(sc, ms)]
    pp = [jnp.exp(p - m) for p, m in zip(sp, ms)]
    den = [jnp.sum(c, axis=-1, keepdims=True) + jnp.sum(p, axis=-1, keepdims=True) for c, p in zip(pc, pp)]
    lse_tile = jnp.zeros((blk, LANES), F32)
    for h, sl in enumerate(heads):
        o = _dot(pc[h].astype(BF16), vc_ref[:, sl]) + _dot(pp[h].astype(BF16), vp_ref[:, sl])
        o_ref[:, sl] = o / den[h]
        lse_tile = jnp.where(col == h, ms[h] + jnp.log(den[h]), lse_tile)
    l_ref[...] = lse_tile


def _attention(q, k, v, dil):
    b, s, w = q.shape
    sub = s // dil
    view = lambda t: t.reshape(b, sub, dil * w)
    cur = pl.BlockSpec((None, ATT_BLOCK, w), lambda bi, r, i: (bi, i, r))
    prev = pl.BlockSpec((None, ATT_BLOCK, w), lambda bi, r, i: (bi, jnp.maximum(i - 1, 0), r))
    o, lse = pl.pallas_call(
        _attn_body,
        grid=(b, dil, sub // ATT_BLOCK),
        in_specs=[cur, cur, prev, cur, prev],
        out_specs=[cur, pl.BlockSpec((None, ATT_BLOCK, LANES), lambda bi, r, i: (bi, i, r))],
        out_shape=[jax.ShapeDtypeStruct((b, sub, dil * w), F32),
                   jax.ShapeDtypeStruct((b, sub, dil * LANES), F32)],
        compiler_params=_params(("parallel", "parallel", "parallel")),
        name=f"attn_d{dil}",
    )(view(q), view(k), view(k), view(v), view(v))
    return o.reshape(b, s, w), lse.reshape(b, s, LANES)


def _gelu_tanh(x):
    return 0.5 * x * (1.0 + jnp.tanh(math.sqrt(2.0 / math.pi) * (x + 0.044715 * (x * x * x))))


def _s5_slab(u_ref, perm_ref):
    u_stack = jnp.concatenate([u_ref[:, sg * WIDTH:(sg + 1) * WIDTH] for sg in range(S5_SEGS)], axis=0)
    u_slab = _dot(perm_ref[...], u_stack.astype(BF16)).astype(BF16)
    return u_stack, u_slab


def _s5_scan_half(u_slab, half, bbd_ref, asame_ref, across_ref, bu_ref, st_ref, store):
    hs = S5_HALF_STATES
    cin = WIDTH // S5_HALVES
    cols = slice(half * 2 * hs, (half + 1) * 2 * hs)
    bu_ref[...] = _dot(u_slab[:, half * cin:(half + 1) * cin], bbd_ref[half])
    a_same = asame_ref[:, cols]
    a_cross = across_ref[:, cols]

    def step(t, x):
        r0 = pl.multiple_of(t * S5_SEGS, S5_SEGS)
        swapped = jnp.concatenate([x[:, hs:], x[:, :hs]], axis=-1)
        x = a_same * x + a_cross * swapped + bu_ref[pl.ds(r0, S5_SEGS), :]
        if store:
            bu_ref[pl.ds(r0, S5_SEGS), :] = x
        return x

    st_ref[:, cols] = lax.fori_loop(0, bu_ref.shape[0] // S5_SEGS, step, st_ref[:, cols], unroll=4)


def _s5_state_body(u_ref, perm_ref, bbd_ref, asame_ref, across_ref, end_ref, bu_ref, st_ref):
    @pl.when(pl.program_id(1) == 0)
    def _():
        st_ref[...] = jnp.zeros_like(st_ref)

    _, u_slab = _s5_slab(u_ref, perm_ref)
    for half in range(S5_HALVES):
        _s5_scan_half(u_slab, half, bbd_ref, asame_ref, across_ref, bu_ref, st_ref, store=False)

    @pl.when(pl.program_id(1) == pl.num_programs(1) - 1)
    def _():
        end_ref[...] = st_ref[...]


def _cmul(z, w):
    hs = z.shape[-1] // 2
    zr, zi, wr, wi = z[:, :hs], z[:, hs:], w[:, :hs], w[:, hs:]
    return jnp.concatenate([zr * wr - zi * wi, zr * wi + zi * wr], axis=-1)


def _s5_out_body(seg_len, u_ref, end_ref, perm_ref, permt_ref, bbd_ref, asame_ref, across_ref, aplain_ref,
                 cbd_ref, d_ref, wglu_ref, bglu_ref, o_ref, bu_ref, st_ref):
    hs = S5_HALF_STATES
    tt = u_ref.shape[0]

    @pl.when(pl.program_id(1) == 0)
    def _():
        for half in range(S5_HALVES):
            cols = slice(half * 2 * hs, (half + 1) * 2 * hs)
            base = aplain_ref[:, cols]
            power = None
            e = seg_len
            while e:
                if e & 1:
                    power = base if power is None else _cmul(power, base)
                e >>= 1
                if e:
                    base = _cmul(base, base)
            x = jnp.zeros((1, 2 * hs), F32)
            for sg in range(S5_SEGS):
                st_ref[sg:sg + 1, cols] = x
                x = _cmul(x, power) + end_ref[sg:sg + 1, cols]

    u_stack, u_slab = _s5_slab(u_ref, perm_ref)
    ys = []
    for half in range(S5_HALVES):
        _s5_scan_half(u_slab, half, bbd_ref, asame_ref, across_ref, bu_ref, st_ref, store=True)
        ys.append(_dot(bu_ref[...].astype(BF16), cbd_ref[half]))
    hi, lo = _split2(jnp.concatenate(ys, axis=-1))
    y = _dot(permt_ref[...], hi) + _dot(permt_ref[...], lo) + d_ref[...] * u_stack
    z = _dot(_gelu_tanh(y).astype(BF16), wglu_ref[...]) + bglu_ref[...]
    out = z[:, :WIDTH] * _sigmoid(z[:, WIDTH:])
    for sg in range(S5_SEGS):
        o_ref[:, sg * WIDTH:(sg + 1) * WIDTH] = out[sg * tt:(sg + 1) * tt]


def _s5(u_view, bbd, a_same, a_cross, a_plain, cbd, d_skip, wglu, bglu, tt=64):
    b, seg_len, wide = u_view.shape
    m = tt * S5_SEGS
    dst = jnp.arange(m)
    src = (dst % S5_SEGS) * tt + dst // S5_SEGS
    perm = (src[:, None] == jnp.arange(m)[None, :]).astype(BF16)
    row = pl.BlockSpec((None, tt, wide), lambda bi, i: (bi, i, 0))
    ends = pl.BlockSpec((None, S5_SEGS, 2 * S5_NSTATE), lambda bi, i: (bi, 0, 0))
    scratch = [pltpu.VMEM((m, 2 * S5_HALF_STATES), F32), pltpu.VMEM((S5_SEGS, 2 * S5_NSTATE), F32)]
    consts1 = (perm, bbd, a_same, a_cross)
    seg_end = pl.pallas_call(
        _s5_state_body,
        grid=(b, seg_len // tt),
        in_specs=[row] + [_const_spec(c.shape) for c in consts1],
        out_specs=ends,
        out_shape=jax.ShapeDtypeStruct((b, S5_SEGS, 2 * S5_NSTATE), F32),
        scratch_shapes=scratch,
        compiler_params=_params(("parallel", "arbitrary")),
        name="s5_state",
    )(u_view, *consts1)
    consts2 = (perm, perm.T, bbd, a_same, a_cross, a_plain, cbd, d_skip, wglu, bglu)
    return pl.pallas_call(
        functools.partial(_s5_out_body, seg_len),
        grid=(b, seg_len // tt),
        in_specs=[row, ends] + [_const_spec(c.shape) for c in consts2],
        out_specs=row,
        out_shape=jax.ShapeDtypeStruct((b, seg_len, wide), F32),
        scratch_shapes=scratch,
        compiler_params=_params(("parallel", "arbitrary")),
        name="s5_out",
    )(u_view, seg_end, *consts2)


V_MU_R, V_MU_K, V_MU_V, V_W0, V_A0, V_V0, V_KK, V_KA, V_RK, V_GNW, V_GNB = range(11)


def _rwkv_body(has_vres, *refs):
    if has_vres:
        (rkv_ref, lora_ref, vf_ref, vec_ref, w2_ref, a2_ref, g2_ref, v2_ref, hsum_ref,
         y_ref, st_ref, carry_ref, ysc_ref) = refs
    else:
        (rkv_ref, lora_ref, vec_ref, w2_ref, a2_ref, g2_ref, hsum_ref,
         y_ref, vout_ref, st_ref, carry_ref, ysc_ref) = refs
    nb, L = rkv_ref.shape[0], rkv_ref.shape[1]
    m = nb * L
    w = WIDTH

    @pl.when(pl.program_id(0) == 0)
    def _():
        st_ref[...] = jnp.zeros_like(st_ref)
        carry_ref[...] = jnp.zeros_like(carry_ref)

    vec = vec_ref[...]
    row = lambda j: vec[j:j + 1, :]
    hsum = hsum_ref[...]

    rkv = rkv_ref[...].reshape(m, 3 * w)
    lora = lora_ref[...].reshape(m, 2 * LORA_HALF)
    cat = jnp.concatenate([rkv, lora[:, LORA_HALF:]], axis=-1)
    row_id = lax.broadcasted_iota(jnp.int32, (m, 1), 0)
    prev = pltpu.roll(cat, 1, 0)
    for b in range(nb):
        prev = jnp.where(row_id == b * L, carry_ref[b], prev)
        carry_ref[b] = cat[(b + 1) * L - 1:(b + 1) * L, :]
    lerp = lambda j, mu: rkv[:, j * w:(j + 1) * w] + (prev[:, j * w:(j + 1) * w] - rkv[:, j * w:(j + 1) * w]) * mu
    r = lerp(0, row(V_MU_R))
    k = lerp(1, row(V_MU_K))
    v = lerp(2, row(V_MU_V))
    pre = lora[:, :LORA_HALF] + prev[:, 3 * w:]

    wx = row(V_W0) + _dot(jnp.tanh(pre[:, 0:64]).astype(BF16), w2_ref[...])
    logw = -jnp.exp(-_softplus(-wx) - 0.5)
    a = _sigmoid(row(V_A0) + _dot(pre[:, 64:128].astype(BF16), a2_ref[...]))
    g = _dot(_sigmoid(pre[:, 128:256]).astype(BF16), g2_ref[...])
    if has_vres:
        vf = vf_ref[...].reshape(m, w)
        v = v + (vf - v) * _sigmoid(row(V_V0) + _dot(pre[:, 256:320].astype(BF16), v2_ref[...]))
    else:
        vout_ref[...] = v.reshape(nb, L, w)

    kk = k * row(V_KK)
    kk = kk * lax.rsqrt(jnp.maximum(_dot_sel(kk * kk, hsum), 1e-24))
    k = k * (1.0 + (a - 1.0) * row(V_KA))
    bonus = _dot_sel(r * k * row(V_RK), hsum) * v

    mi = lax.broadcasted_iota(jnp.int32, (m, m), 0)
    mj = lax.broadcasted_iota(jnp.int32, (m, m), 1)
    same_batch = (mi // L) == (mj // L)
    tri = jnp.where(jnp.logical_and(same_batch, mj <= mi), 1.0, 0.0).astype(BF16)
    hi, mid, lo = _split3(logw)
    cl = _dot(tri, hi) + _dot(tri, mid) + _dot(tri, lo)
    e_neg = jnp.exp(-cl)
    a_f = (-kk) * jnp.exp(cl - logw)
    a_t = a_f.astype(BF16)
    b_t = ((kk * a) * e_neg).astype(BF16)
    k_t = (k * e_neg).astype(BF16)
    r_t = (r * jnp.exp(cl)).astype(BF16)
    v_b = v.astype(BF16)

    ti = lax.broadcasted_iota(jnp.int32, (L, L), 0)
    tj = lax.broadcasted_iota(jnp.int32, (L, L), 1)
    upper_strict = ti < tj
    lower_incl = tj <= ti
    eye = (ti == tj).astype(F32)
    n_double = int(math.log2(L)) - 1

    chains = []
    for b in range(nb):
        rows = slice(b * L, (b + 1) * L)
        a_tt = a_f[rows].T.astype(BF16)
        v_tt = v[rows].T.astype(BF16)
        p_end = jnp.exp(cl[(b + 1) * L - 1:(b + 1) * L, :])
        for h in range(N_HEADS):
            sl = slice(h * HEAD_DIM, (h + 1) * HEAD_DIM)
            chains.append(dict(b=b, h=h, rows=rows, sl=sl, a=a_t[rows, sl], bb=b_t[rows, sl], k=k_t[rows, sl],
                               r=r_t[rows, sl], v=v_b[rows, sl], att=a_tt[sl, :], vtt=v_tt[sl, :],
                               p_end=p_end[:, sl]))
    for c in chains:
        c["tab"] = jnp.where(upper_strict, _dot_nt(c["bb"], c["a"]), 0.0)
        c["tak"] = jnp.where(upper_strict, _dot_nt(c["k"], c["a"]), 0.0).astype(BF16)
        c["trb"] = jnp.where(lower_incl, _dot_nt(c["r"], c["bb"]), 0.0).astype(BF16)
        c["trk"] = jnp.where(lower_incl, _dot_nt(c["r"], c["k"]), 0.0).astype(BF16)
    for c in chains:
        c["inv"] = eye + c["tab"]
        c["pw"] = c["tab"].astype(BF16)
    for step in range(n_double):
        for c in chains:
            pw = _dot(c["pw"], c["pw"])
            c["pw"] = pw.astype(BF16)
        for c in chains:
            c["inv"] = c["inv"] + _dot(c["inv"].astype(BF16), c["pw"])
    for c in chains:
        c["st"] = st_ref[c["b"], c["h"]]
        c["stb"] = c["st"].astype(BF16)
        c["rhs"] = _dot(c["stb"], c["att"]) + _dot(c["vtt"], c["tak"])
    for c in chains:
        c["ut"] = _dot(c["rhs"].astype(BF16), c["inv"].astype(BF16)).astype(BF16)
    for c in chains:
        st_ref[c["b"], c["h"]] = (c["st"] + _dot(c["ut"], c["bb"]) + _dot(c["vtt"], c["k"])) * c["p_end"]
    for c in chains:
        ysc_ref[c["rows"], c["sl"]] = (_dot_nt(c["r"], c["stb"]) + _dot_nt(c["trb"], c["ut"])
                                       + _dot(c["trk"], c["v"]))

    y = ysc_ref[...]
    mean = _dot_sel(y, hsum) * (1.0 / HEAD_DIM)
    dev = y - mean
    var = _dot_sel(dev * dev, hsum) * (1.0 / HEAD_DIM)
    y = dev * lax.rsqrt(var + RWKV_GN_EPS) * row(V_GNW) + row(V_GNB)
    y_ref[...] = ((y + bonus) * g).reshape(nb, L, w)


def _rwkv(rkv, lora, v_first, vec, w2, a2, g2, v2, hsum):
    b, s, _ = rkv.shape
    L = RWKV_CHUNK
    has_vres = v_first is not None
    row = lambda wd: pl.BlockSpec((b, L, wd), lambda i: (0, i, 0))
    ins = [rkv, lora] + ([v_first] if has_vres else []) + [vec, w2, a2, g2] + ([v2] if has_vres else []) + [hsum]
    in_specs = [row(3 * WIDTH), row(2 * LORA_HALF)] + ([row(WIDTH)] if has_vres else [])
    in_specs += [_const_spec(c.shape) for c in ins[len(in_specs):]]
    y_shape = jax.ShapeDtypeStruct((b, s, WIDTH), F32)
    out = pl.pallas_call(
        functools.partial(_rwkv_body, has_vres),
        grid=(s // L,),
        in_specs=in_specs,
        out_specs=row(WIDTH) if has_vres else [row(WIDTH), row(WIDTH)],
        out_shape=y_shape if has_vres else [y_shape, y_shape],
        scratch_shapes=[pltpu.VMEM((b, N_HEADS, HEAD_DIM, HEAD_DIM), F32),
                        pltpu.VMEM((b, 1, 3 * WIDTH + LORA_HALF), F32),
                        pltpu.VMEM((b * L, WIDTH), F32)],
        compiler_params=_params(("arbitrary",)),
        name="rwkv_l1" if has_vres else "rwkv_l0",
    )(*ins)
    return (out, v_first) if has_vres else (out[0], out[1])


def _lru_body(x_ref, cw_ref, cb_ref, wa_ref, ba_ref, wx_ref, bx_ref, lam_ref, o_ref, a_ref, carry_ref, h_ref):
    tb = x_ref.shape[0]

    @pl.when(pl.program_id(1) == 0)
    def _():
        carry_ref[...] = jnp.zeros_like(carry_ref)
        h_ref[...] = jnp.zeros_like(h_ref)

    x = x_ref[...]
    full = jnp.concatenate([carry_ref[...], x], axis=0)
    carry_ref[...] = x[tb - SUBLANES:]
    cw = cw_ref[...]
    taps = cw.shape[0]
    xc = cb_ref[...] + cw[taps - 1:taps, :] * x
    for j in range(1, taps):
        xc = xc + cw[taps - 1 - j:taps - j, :] * pltpu.roll(full, j, 0)[SUBLANES:]
    xcb = xc.astype(BF16)
    r = _sigmoid(_dot(xcb, wa_ref[...]) + ba_ref[...])
    i = _sigmoid(_dot(xcb, wx_ref[...]) + bx_ref[...])
    log_a = -LRU_C * r * _softplus(-lam_ref[...])
    a = jnp.exp(log_a)
    a_ref[...] = a
    o_ref[...] = jnp.sqrt(-jnp.tanh(log_a) * (a * a + 1.0)) * (i * xc)

    def step(t, h):
        h = a_ref[pl.ds(t, 1), :] * h + o_ref[pl.ds(t, 1), :]
        o_ref[pl.ds(t, 1), :] = h
        return h

    h_ref[0:1, :] = lax.fori_loop(0, tb, step, h_ref[0:1, :], unroll=8)


def _lru(x, cw, cb, wa, ba, wx, bx, lam, tb=512):
    b, s, w = x.shape
    row = pl.BlockSpec((None, tb, w), lambda bi, i: (bi, i, 0))
    consts = (cw, cb, wa, ba, wx, bx, lam)
    return pl.pallas_call(
        _lru_body,
        grid=(b, s // tb),
        in_specs=[row] + [_const_spec(c.shape) for c in consts],
        out_specs=row,
        out_shape=jax.ShapeDtypeStruct((b, s, w), F32),
        scratch_shapes=[pltpu.VMEM((tb, w), F32), pltpu.VMEM((SUBLANES, w), F32), pltpu.VMEM((SUBLANES, w), F32)],
        compiler_params=_params(("parallel", "arbitrary")),
        name="lru",
    )(x, *consts)


def _merge_body(x_ref, h_ref, o1_ref, o2_ref, o3_ref, l1_ref, l2_ref, l3_ref, yb_ref, yc_ref, yd_ref,
                wg_ref, wb_ref, wo_ref, exp_ref, out_ref):
    hb = h_ref[...]
    l1, l2, l3 = l1_ref[...], l2_ref[...], l3_ref[...]
    m = jnp.maximum(jnp.maximum(l1, l2), l3)
    e1, e2, e3 = jnp.exp(l1 - m), jnp.exp(l2 - m), jnp.exp(l3 - m)
    inv = 1.0 / (e1 + e2 + e3)
    expand = exp_ref[...]
    ya = (_dot_sel(e1 * inv, expand) * o1_ref[...] + _dot_sel(e2 * inv, expand) * o2_ref[...]
          + _dot_sel(e3 * inv, expand) * o3_ref[...])
    merged = None
    for n, y in enumerate((ya, yb_ref[...], yc_ref[...], yd_ref[...])):
        gate = _sigmoid(_dot(hb, wg_ref[:, n * D_MODEL:(n + 1) * D_MODEL]))
        term = gate * _dot(y.astype(BF16), wb_ref[n])
        merged = term if merged is None else merged + term
    out_ref[...] = x_ref[...] + _dot(merged.astype(BF16), wo_ref[...])


def _merge(x, h, os, ls, yb, yc, yd, wg, wb, wo, expand, tm=256):
    b, s, d = x.shape
    row = lambda wd: pl.BlockSpec((None, tm, wd), lambda bi, i: (bi, i, 0))
    consts = (wg, wb, wo, expand)
    return pl.pallas_call(
        _merge_body,
        grid=(b, s // tm),
        in_specs=[row(d), row(d)] + [row(WIDTH)] * 3 + [row(LANES)] * 3
        + [_seg_view_spec(s, tm), row(WIDTH), row(WIDTH)] + [_const_spec(c.shape) for c in consts],
        out_specs=row(d),
        out_shape=jax.ShapeDtypeStruct((b, s, d), F32),
        compiler_params=_params(("parallel", "parallel")),
        name="merge",
    )(x, h, *os, *ls, yb, yc, yd, *consts)


def _ffn_body(final, x_ref, g_ref, wg_ref, wu_ref, wd_ref, gf_ref, out_ref):
    x = x_ref[...]
    hb = _rms(x, g_ref[...]).astype(BF16)
    gate = _dot(hb, wg_ref[...])
    up = _dot(hb, wu_ref[...])
    y = x + _dot((gate * _sigmoid(gate) * up).astype(BF16), wd_ref[...])
    out_ref[...] = _rms(y, gf_ref[...]) if final else y


def _ffn(x, g, wg, wu, wd, gf, final, tm=256):
    b, s, d = x.shape
    row = pl.BlockSpec((None, tm, d), lambda bi, i: (bi, i, 0))
    consts = (g, wg, wu, wd, gf)
    return pl.pallas_call(
        functools.partial(_ffn_body, final),
        grid=(b, s // tm),
        in_specs=[row] + [_const_spec(c.shape) for c in consts],
        out_specs=row,
        out_shape=jax.ShapeDtypeStruct((b, s, d), F32),
        compiler_params=_params(("parallel", "parallel")),
        name="ffn_final" if final else "ffn",
    )(x, *consts)


def _rope_tables(s):
    half = HEAD_DIM // 2
    inv = 1.0 / (ROPE_THETA ** (jnp.arange(0, HEAD_DIM, 2, dtype=F32) / HEAD_DIM))
    ang = jnp.arange(s, dtype=F32)[:, None] * inv[None, :]
    cos, sin = jnp.cos(ang), jnp.sin(ang)
    reps = LANES // HEAD_DIM
    cos_t = jnp.tile(jnp.concatenate([cos, cos], axis=-1), (1, reps))
    sin_t = jnp.tile(jnp.concatenate([-sin, sin], axis=-1), (1, reps))
    return cos_t, sin_t


def _block_diag(blocks):
    g, r, c = blocks.shape
    eye = jnp.eye(g, dtype=blocks.dtype)
    return (eye[:, None, :, None] * blocks[:, :, None, :]).reshape(g * r, g * c)


def _s5_params(lam_re, lam_im, log_dt, b_re, b_im, c_re, c_im):
    lr, li = lam_re.astype(F32), lam_im.astype(F32)
    dt = jnp.exp(log_dt.astype(F32))[:, None]
    mag = jnp.exp(lr * dt)
    ab_re, ab_im = mag * jnp.cos(li * dt), mag * jnp.sin(li * dt)
    nr = ab_re - 1.0
    den = lr * lr + li * li
    f_re = (nr * lr + ab_im * li) / den
    f_im = (ab_im * lr - nr * li) / den
    bb_re = f_re[..., None] * b_re - f_im[..., None] * b_im
    bb_im = f_re[..., None] * b_im + f_im[..., None] * b_re
    gh = S5_GROUPS // S5_HALVES
    swap = lambda t: jnp.transpose(t, (0, 2, 1))
    halves = lambda t: [t[h * gh:(h + 1) * gh] for h in range(S5_HALVES)]
    bbd = jnp.stack([jnp.concatenate([_block_diag(swap(re)), _block_diag(swap(im))], axis=-1)
                     for re, im in zip(halves(bb_re), halves(bb_im))]).astype(BF16)
    cbd = jnp.stack([jnp.concatenate([_block_diag(swap(re)), -_block_diag(swap(im))], axis=0)
                     for re, im in zip(halves(c_re), halves(c_im))]).astype(BF16)
    flat = lambda t: [h.reshape(1, -1) for h in halves(t)]
    cat = lambda parts: jnp.concatenate(parts, axis=-1)
    a_plain = cat([cat([re, im]) for re, im in zip(flat(ab_re), flat(ab_im))])
    a_same = cat([cat([re, re]) for re in flat(ab_re)])
    a_cross = cat([cat([-im, im]) for im in flat(ab_im)])
    rep = lambda t: jnp.broadcast_to(t, (S5_SEGS, t.shape[-1]))
    return bbd, rep(a_same), rep(a_cross), a_plain, cbd


def _lora_weights(mu_wag, w1, a1, g1, mu_v, v1):
    d = w1.shape[0]
    pad = lambda t: jnp.pad(t, ((0, 0), (0, LORA_HALF - t.shape[1])))
    v1 = jnp.zeros((d, 32), F32) if v1 is None else v1
    mu_v = jnp.zeros((d,), F32) if mu_v is None else mu_v
    mats = (w1, a1, g1, v1)
    mus = (mu_wag[0], mu_wag[1], mu_wag[2], mu_v)
    keep = pad(jnp.concatenate([m * (1.0 - mu)[:, None] for m, mu in zip(mats, mus)], axis=1))
    shifted = pad(jnp.concatenate([m * mu[:, None] for m, mu in zip(mats, mus)], axis=1))
    return jnp.concatenate([keep, shifted], axis=1).astype(BF16)


def kernel(x, norm_mix, w_in, s5_lam_re, s5_lam_im, s5_log_dt, s5_b_re, s5_b_im, s5_c_re, s5_c_im, s5_d, s5_w_glu, s5_b_glu, rw_mu_rkv, rw_mu_wag, rw_w0, rw_w1, rw_w2, rw_a0, rw_a1, rw_a2, rw_g1, rw_g2, rw_k_k, rw_k_a, rw_r_k, rw_gn_w, rw_gn_b, rw_mu_v, rw_v0, rw_v1, rw_v2, lru_conv_w, lru_conv_b, lru_w_a, lru_b_a, lru_w_x, lru_b_x, lru_lam, w_branch, w_out, norm_ffn, w_ffn_gate, w_ffn_up, w_ffn_down, norm_final):
    b, s, d = x.shape
    depth = w_in.shape[0]
    w = WIDTH
    cos_t, sin_t = _rope_tables(s)
    head_id = jnp.arange(w) // HEAD_DIM
    hsum = (head_id[:, None] == head_id[None, :]).astype(BF16)
    expand = (jnp.arange(LANES)[:, None] == head_id[None, :]).astype(BF16)
    bf = lambda t: t.astype(BF16)
    row = lambda t: t.reshape(1, -1).astype(F32)

    v_first = None
    for l in range(depth):
        wl = w_in[l]
        cols = np.cumsum((0, w, w, w, w, w, w, w, w))
        seg = lambda i, j: bf(wl[:, cols[i]:cols[j]])
        has_vres = l > 0
        wlora = _lora_weights(rw_mu_wag[l], rw_w1[l], rw_a1[l], rw_g1[l],
                              rw_mu_v[l - 1] if has_vres else None, rw_v1[l - 1] if has_vres else None)
        h, q, k, v, u, rkv, xlru, lora = _inproj(
            x, row(norm_mix[l]), cos_t, sin_t, seg(0, 2), seg(2, 3), seg(3, 4), seg(4, 7), seg(7, 8), wlora)

        att = [_attention(q, k, v, dil) for dil in DILATIONS]

        s5p = _s5_params(s5_lam_re[l], s5_lam_im[l], s5_log_dt[l], s5_b_re[l], s5_b_im[l], s5_c_re[l], s5_c_im[l])
        y_b = _s5(u, *s5p, row(s5_d[l]), bf(s5_w_glu[l]), row(s5_b_glu[l]))

        zero = jnp.zeros((w,), F32)
        vec = jnp.stack([rw_mu_rkv[l, 0], rw_mu_rkv[l, 1], rw_mu_rkv[l, 2], rw_w0[l], rw_a0[l],
                         rw_v0[l - 1] if has_vres else zero, rw_k_k[l], rw_k_a[l], rw_r_k[l].reshape(-1),
                         rw_gn_w[l], rw_gn_b[l]] + [zero] * 5).astype(F32)
        v2 = bf(jnp.pad(rw_v2[l - 1], ((0, 32), (0, 0)))) if has_vres else None
        y_c, v_c = _rwkv(rkv, lora, v_first, vec, bf(rw_w2[l]), bf(rw_a2[l]), bf(rw_g2[l]), v2, hsum)
        if l == 0:
            v_first = v_c

        y_d = _lru(xlru, lru_conv_w[l].astype(F32), row(lru_conv_b[l]), bf(_block_diag(lru_w_a[l])), row(lru_b_a[l]),
                   bf(_block_diag(lru_w_x[l])), row(lru_b_x[l]), row(lru_lam[l]))

        x = _merge(x, h, [o for o, _ in att], [ls for _, ls in att], y_b, y_c, y_d,
                   bf(wl[:, cols[8]:]), bf(w_branch[l]), bf(w_out[l]), expand)
        x = _ffn(x, row(norm_ffn[l]), bf(w_ffn_gate[l]), bf(w_ffn_up[l]), bf(w_ffn_down[l]), row(norm_final),
                 final=(l == depth - 1))
    return x
```

```python
import functools
import math

import jax
import jax.numpy as jnp
import numpy as np
from jax import lax
from jax.experimental import pallas as pl
from jax.experimental.pallas import tpu as pltpu

F32 = jnp.float32
BF16 = jnp.bfloat16

D_MODEL = 1024
N_HEADS = 8
HEAD_DIM = 64
WIDTH = N_HEADS * HEAD_DIM
DILATIONS = (1, 4, 16)
ATT_BLOCK = 128
ROPE_THETA = 10000.0
S5_GROUPS = 32
S5_GROUP = 16
S5_STATE = 64
S5_NSTATE = S5_GROUPS * S5_STATE
RWKV_GN_EPS = 64e-5
RWKV_CHUNK = 64
LORA_HALF = 384
LRU_BLOCKS = 8
LRU_C = 8.0
D_FF = 2816
RMS_EPS = 1e-6
NEG_BIG = -1e30

LANES = 128
SUBLANES = 8
S5_SEGS = SUBLANES
S5_HALVES = 2
S5_HALF_STATES = S5_NSTATE // S5_HALVES
VMEM_LIMIT = 56 * 1024 * 1024

NT_DIMS = (((1,), (1,)), ((), ()))


def _dot(a, b):
    return jnp.dot(a, b, preferred_element_type=F32)


def _dot_nt(a, b):
    return lax.dot_general(a, b, NT_DIMS, preferred_element_type=F32)


def _split2(x):
    hi = x.astype(BF16)
    lo = (x - hi.astype(F32)).astype(BF16)
    return hi, lo


def _split3(x):
    hi = x.astype(BF16)
    r1 = x - hi.astype(F32)
    mid = r1.astype(BF16)
    lo = (r1 - mid.astype(F32)).astype(BF16)
    return hi, mid, lo


def _dot_sel(x, sel):
    hi, lo = _split2(x)
    return _dot(hi, sel) + _dot(lo, sel)


def _dot3(a, b):
    ah, al = _split2(a)
    bh, bl = _split2(b)
    return _dot(ah, bh) + _dot(ah, bl) + _dot(al, bh)


def _sigmoid(x):
    return 1.0 / (1.0 + jnp.exp(-x))


def _softplus(z):
    return jnp.maximum(z, 0.0) + jnp.log1p(jnp.exp(-jnp.abs(z)))


def _rms(x, g):
    ms = jnp.mean(x * x, axis=-1, keepdims=True)
    return x * lax.rsqrt(ms + RMS_EPS) * g


def _const_spec(shape):
    n = len(shape)
    return pl.BlockSpec(shape, lambda *_: (0,) * n)


def _params(sem):
    return pltpu.CompilerParams(dimension_semantics=sem, vmem_limit_bytes=VMEM_LIMIT)


def _seg_view_spec(s, tm):
    tiles_per_seg = s // S5_SEGS // tm
    return pl.BlockSpec((None, tm, WIDTH), lambda bi, i: (bi, i % tiles_per_seg, i // tiles_per_seg))


def _dilated_perm(tm, dil):
    dst = jnp.arange(tm)
    src = (dst % (tm // dil)) * dil + dst // (tm // dil)
    return (src[:, None] == jnp.arange(tm)[None, :]).astype(BF16)


def _inproj_body(x_ref, g_ref, cos_ref, sin_ref, wqk_ref, wv_ref, wu_ref, wrkv_ref, wlru_ref, wlora_ref, *rest):
    n_dil = len(DILATIONS) - 1
    perm_refs, (h_ref, u_ref, rkv_ref, lru_ref, lora_ref, qkv_ref), dil_refs = rest[:n_dil], rest[n_dil:n_dil + 6], rest[n_dil + 6:]
    tm = x_ref.shape[0]
    hb = _rms(x_ref[...], g_ref[...]).astype(BF16)
    h_ref[...] = hb
    qk = _dot(hb, wqk_ref[...])
    reps = 2 * WIDTH // LANES
    cos = jnp.concatenate([cos_ref[...]] * reps, axis=-1)
    sin = jnp.concatenate([sin_ref[...]] * reps, axis=-1)
    lane = lax.broadcasted_iota(jnp.int32, qk.shape, 1)
    first_half = (lane % HEAD_DIM) < (HEAD_DIM // 2)
    partner = jnp.where(first_half,
                        pltpu.roll(qk, 2 * WIDTH - HEAD_DIM // 2, 1),
                        pltpu.roll(qk, HEAD_DIM // 2, 1))
    rot = qk * cos + partner * sin
    qkv = jnp.concatenate([(rot[:, :WIDTH] * (HEAD_DIM ** -0.5)).astype(BF16), rot[:, WIDTH:].astype(BF16),
                           _dot(hb, wv_ref[...]).astype(BF16)], axis=-1)
    qkv_ref[...] = qkv
    for dil, perm_ref, out_ref in zip(DILATIONS[1:], perm_refs, dil_refs):
        grouped = _dot(perm_ref[...], qkv).astype(BF16)
        rows = tm // dil
        for r in range(dil):
            out_ref[:, r * 3 * WIDTH:(r + 1) * 3 * WIDTH] = grouped[r * rows:(r + 1) * rows]
    u_ref[...] = _dot(hb, wu_ref[...])
    rkv_ref[...] = _dot(hb, wrkv_ref[...])
    lru_ref[...] = _dot(hb, wlru_ref[...])
    lora_ref[...] = _dot(hb, wlora_ref[...])


def _inproj(x, g, cos, sin, wqk, wv, wu, wrkv, wlru, wlora, tm=256):
    b, s, d = x.shape
    grid = (b, s // tm)
    row = lambda w: pl.BlockSpec((None, tm, w), lambda bi, i: (bi, i, 0))
    tab = pl.BlockSpec((tm, LANES), lambda bi, i: (i, 0))
    widths = (d, WIDTH, 3 * WIDTH, WIDTH, 2 * LORA_HALF, 3 * WIDTH)
    dtypes = (BF16, F32, F32, F32, F32, BF16)
    out_specs = [row(w) for w in widths]
    out_shape = [jax.ShapeDtypeStruct((b, s, w), dt) for w, dt in zip(widths, dtypes)]
    out_specs[1] = _seg_view_spec(s, tm)
    out_shape[1] = jax.ShapeDtypeStruct((b, s // S5_SEGS, S5_SEGS * WIDTH), F32)
    for dil in DILATIONS[1:]:
        out_specs.append(pl.BlockSpec((None, tm // dil, dil * 3 * WIDTH), lambda bi, i: (bi, i, 0)))
        out_shape.append(jax.ShapeDtypeStruct((b, s // dil, dil * 3 * WIDTH), BF16))
    consts = (wqk, wv, wu, wrkv, wlru, wlora) + tuple(_dilated_perm(tm, dil) for dil in DILATIONS[1:])
    return pl.pallas_call(
        _inproj_body,
        grid=grid,
        in_specs=[row(d), _const_spec((1, d)), tab, tab] + [_const_spec(c.shape) for c in consts],
        out_specs=out_specs,
        out_shape=out_shape,
        compiler_params=_params(("parallel", "parallel")),
        name="inproj",
    )(x, g, cos, sin, *consts)


def _attn_body(q_ref, kc_ref, kp_ref, vc_ref, vp_ref, o_ref, l_ref):
    i = pl.program_id(2)
    blk = ATT_BLOCK
    row = lax.broadcasted_iota(jnp.int32, (blk, blk), 0)
    col = lax.broadcasted_iota(jnp.int32, (blk, blk), 1)
    mask_c = col <= row
    mask_p = jnp.logical_and(col >= row, i > 0)
    heads = [slice(h * HEAD_DIM, (h + 1) * HEAD_DIM) for h in range(N_HEADS)]
    qs = [q_ref[:, sl] for sl in heads]
    sc = [jnp.where(mask_c, _dot_nt(q, kc_ref[:, sl]), NEG_BIG) for q, sl in zip(qs, heads)]
    sp = [jnp.where(mask_p, _dot_nt(q, kp_ref[:, sl]), NEG_BIG) for q, sl in zip(qs, heads)]
    ms = [jnp.maximum(jnp.max(c, axis=-1, keepdims=True), jnp.max(p, axis=-1, keepdims=True))
          for c, p in zip(sc, sp)]
    pc = [jnp.exp(c - m) for c, m in zip(sc, ms)]
    pp = [jnp.exp(p - m) for p, m in zip(sp, ms)]
    den = [jnp.sum(c, axis=-1, keepdims=True) + jnp.sum(p, axis=-1, keepdims=True) for c, p in zip(pc, pp)]
    lse_tile = jnp.zeros((blk, LANES), F32)
    for h, sl in enumerate(heads):
        o = _dot(pc[h].astype(BF16), vc_ref[:, sl]) + _dot(pp[h].astype(BF16), vp_ref[:, sl])
        o_ref[:, sl] = (o / den[h]).astype(BF16)
        lse_tile = jnp.where(col == h, ms[h] + jnp.log(den[h]), lse_tile)
    l_ref[...] = lse_tile


def _attention(qkv_view, dil):
    b, sub, _ = qkv_view.shape
    w = WIDTH
    cur = lambda j: pl.BlockSpec((None, ATT_BLOCK, w), lambda bi, r, i: (bi, i, 3 * r + j))
    prev = lambda j: pl.BlockSpec((None, ATT_BLOCK, w), lambda bi, r, i: (bi, jnp.maximum(i - 1, 0), 3 * r + j))
    return pl.pallas_call(
        _attn_body,
        grid=(b, dil, sub // ATT_BLOCK),
        in_specs=[cur(0), cur(1), prev(1), cur(2), prev(2)],
        out_specs=[pl.BlockSpec((None, ATT_BLOCK, w), lambda bi, r, i: (bi, i, r)),
                   pl.BlockSpec((None, ATT_BLOCK, LANES), lambda bi, r, i: (bi, i, r))],
        out_shape=[jax.ShapeDtypeStruct((b, sub, dil * w), BF16),
                   jax.ShapeDtypeStruct((b, sub, dil * LANES), F32)],
        compiler_params=_params(("parallel", "parallel", "parallel")),
        name=f"attn_d{dil}",
    )(*([qkv_view] * 5))


def _gelu_tanh(x):
    return 0.5 * x * (1.0 + jnp.tanh(math.sqrt(2.0 / math.pi) * (x + 0.044715 * (x * x * x))))


def _s5_slab(u_ref, perm_ref):
    u_stack = jnp.concatenate([u_ref[:, sg * WIDTH:(sg + 1) * WIDTH] for sg in range(S5_SEGS)], axis=0)
    u_slab = _dot(perm_ref[...], u_stack.astype(BF16)).astype(BF16)
    return u_stack, u_slab


def _s5_scan_half(u_slab, half, bbd_ref, asame_ref, across_ref, bu_ref, st_ref, store):
    hs = S5_HALF_STATES
    cin = WIDTH // S5_HALVES
    cols = slice(half * 2 * hs, (half + 1) * 2 * hs)
    bu_ref[...] = _dot(u_slab[:, half * cin:(half + 1) * cin], bbd_ref[half])
    a_same = asame_ref[:, cols]
    a_cross = across_ref[:, cols]

    def step(t, x):
        r0 = pl.multiple_of(t * S5_SEGS, S5_SEGS)
        swapped = jnp.concatenate([x[:, hs:], x[:, :hs]], axis=-1)
        x = a_same * x + a_cross * swapped + bu_ref[pl.ds(r0, S5_SEGS), :]
        if store:
            bu_ref[pl.ds(r0, S5_SEGS), :] = x
        return x

    st_ref[:, cols] = lax.fori_loop(0, bu_ref.shape[0] // S5_SEGS, step, st_ref[:, cols], unroll=4)


def _s5_state_body(u_ref, perm_ref, bbd_ref, asame_ref, across_ref, end_ref, bu_ref, st_ref):
    @pl.when(pl.program_id(1) == 0)
    def _():
        st_ref[...] = jnp.zeros_like(st_ref)

    _, u_slab = _s5_slab(u_ref, perm_ref)
    for half in range(S5_HALVES):
        _s5_scan_half(u_slab, half, bbd_ref, asame_ref, across_ref, bu_ref, st_ref, store=False)

    @pl.when(pl.program_id(1) == pl.num_programs(1) - 1)
    def _():
        end_ref[...] = st_ref[...]


def _cmul(z, w):
    hs = z.shape[-1] // 2
    zr, zi, wr, wi = z[:, :hs], z[:, hs:], w[:, :hs], w[:, hs:]
    return jnp.concatenate([zr * wr - zi * wi, zr * wi + zi * wr], axis=-1)


def _s5_out_body(seg_len, u_ref, end_ref, perm_ref, permt_ref, bbd_ref, asame_ref, across_ref, aplain_ref,
                 cbd_ref, d_ref, wglu_ref, bglu_ref, o_ref, bu_ref, st_ref):
    hs = S5_HALF_STATES
    tt = u_ref.shape[0]

    @pl.when(pl.program_id(1) == 0)
    def _():
        for half in range(S5_HALVES):
            cols = slice(half * 2 * hs, (half + 1) * 2 * hs)
            base = aplain_ref[:, cols]
            power = None
            e = seg_len
            while e:
                if e & 1:
                    power = base if power is None else _cmul(power, base)
                e >>= 1
                if e:
                    base = _cmul(base, base)
            x = jnp.zeros((1, 2 * hs), F32)
            for sg in range(S5_SEGS):
                st_ref[sg:sg + 1, cols] = x
                x = _cmul(x, power) + end_ref[sg:sg + 1, cols]

    u_stack, u_slab = _s5_slab(u_ref, perm_ref)
    ys = []
    for half in range(S5_HALVES):
        _s5_scan_half(u_slab, half, bbd_ref, asame_ref, across_ref, bu_ref, st_ref, store=True)
        ys.append(_dot(bu_ref[...].astype(BF16), cbd_ref[half]))
    hi, lo = _split2(jnp.concatenate(ys, axis=-1))
    y = _dot(permt_ref[...], hi) + _dot(permt_ref[...], lo) + d_ref[...] * u_stack
    z = _dot(_gelu_tanh(y).astype(BF16), wglu_ref[...]) + bglu_ref[...]
    out = z[:, :WIDTH] * _sigmoid(z[:, WIDTH:])
    for sg in range(S5_SEGS):
        o_ref[:, sg * WIDTH:(sg + 1) * WIDTH] = out[sg * tt:(sg + 1) * tt]


def _s5(u_view, bbd, a_same, a_cross, a_plain, cbd, d_skip, wglu, bglu, tt=64):
    b, seg_len, wide = u_view.shape
    m = tt * S5_SEGS
    dst = jnp.arange(m)
    src = (dst % S5_SEGS) * tt + dst // S5_SEGS
    perm = (src[:, None] == jnp.arange(m)[None, :]).astype(BF16)
    row = pl.BlockSpec((None, tt, wide), lambda bi, i: (bi, i, 0))
    ends = pl.BlockSpec((None, S5_SEGS, 2 * S5_NSTATE), lambda bi, i: (bi, 0, 0))
    scratch = [pltpu.VMEM((m, 2 * S5_HALF_STATES), F32), pltpu.VMEM((S5_SEGS, 2 * S5_NSTATE), F32)]
    consts1 = (perm, bbd, a_same, a_cross)
    seg_end = pl.pallas_call(
        _s5_state_body,
        grid=(b, seg_len // tt),
        in_specs=[row] + [_const_spec(c.shape) for c in consts1],
        out_specs=ends,
        out_shape=jax.ShapeDtypeStruct((b, S5_SEGS, 2 * S5_NSTATE), F32),
        scratch_shapes=scratch,
        compiler_params=_params(("parallel", "arbitrary")),
        name="s5_state",
    )(u_view, *consts1)
    consts2 = (perm, perm.T, bbd, a_same, a_cross, a_plain, cbd, d_skip, wglu, bglu)
    return pl.pallas_call(
        functools.partial(_s5_out_body, seg_len),
        grid=(b, seg_len // tt),
        in_specs=[row, ends] + [_const_spec(c.shape) for c in consts2],
        out_specs=row,
        out_shape=jax.ShapeDtypeStruct((b, seg_len, wide), F32),
        scratch_shapes=scratch,
        compiler_params=_params(("parallel", "arbitrary")),
        name="s5_out",
    )(u_view, seg_end, *consts2)


V_MU_R, V_MU_K, V_MU_V, V_W0, V_A0, V_V0, V_KK, V_KA, V_RK, V_GNW, V_GNB = range(11)


def _rwkv_body(has_vres, *refs):
    if has_vres:
        (rkv_ref, lora_ref, vf_ref, vec_ref, w2_ref, a2_ref, g2_ref, v2_ref, hsum_ref,
         y_ref, st_ref, carry_ref, ysc_ref) = refs
    else:
        (rkv_ref, lora_ref, vec_ref, w2_ref, a2_ref, g2_ref, hsum_ref,
         y_ref, vout_ref, st_ref, carry_ref, ysc_ref) = refs
    nb, L = rkv_ref.shape[0], rkv_ref.shape[1]
    m = nb * L
    w = WIDTH

    @pl.when(pl.program_id(0) == 0)
    def _():
        st_ref[...] = jnp.zeros_like(st_ref)
        carry_ref[...] = jnp.zeros_like(carry_ref)

    vec = vec_ref[...]
    row = lambda j: vec[j:j + 1, :]
    hsum = hsum_ref[...]

    rkv = rkv_ref[...].reshape(m, 3 * w)
    lora = lora_ref[...].reshape(m, 2 * LORA_HALF)
    cat = jnp.concatenate([rkv, lora[:, LORA_HALF:]], axis=-1)
    row_id = lax.broadcasted_iota(jnp.int32, (m, 1), 0)
    prev = pltpu.roll(cat, 1, 0)
    for b in range(nb):
        prev = jnp.where(row_id == b * L, carry_ref[b], prev)
        carry_ref[b] = cat[(b + 1) * L - 1:(b + 1) * L, :]
    lerp = lambda j, mu: rkv[:, j * w:(j + 1) * w] + (prev[:, j * w:(j + 1) * w] - rkv[:, j * w:(j + 1) * w]) * mu
    r = lerp(0, row(V_MU_R))
    k = lerp(1, row(V_MU_K))
    v = lerp(2, row(V_MU_V))
    pre = lora[:, :LORA_HALF] + prev[:, 3 * w:]

    wx = row(V_W0) + _dot(jnp.tanh(pre[:, 0:64]).astype(BF16), w2_ref[...])
    logw = -jnp.exp(-_softplus(-wx) - 0.5)
    a = _sigmoid(row(V_A0) + _dot(pre[:, 64:128].astype(BF16), a2_ref[...]))
    g = _dot(_sigmoid(pre[:, 128:256]).astype(BF16), g2_ref[...])
    if has_vres:
        vf = vf_ref[...].reshape(m, w)
        v = v + (vf - v) * _sigmoid(row(V_V0) + _dot(pre[:, 256:320].astype(BF16), v2_ref[...]))
    else:
        vout_ref[...] = v.reshape(nb, L, w)

    kk = k * row(V_KK)
    kk = kk * lax.rsqrt(jnp.maximum(_dot_sel(kk * kk, hsum), 1e-24))
    k = k * (1.0 + (a - 1.0) * row(V_KA))
    bonus = _dot_sel(r * k * row(V_RK), hsum) * v

    mi = lax.broadcasted_iota(jnp.int32, (m, m), 0)
    mj = lax.broadcasted_iota(jnp.int32, (m, m), 1)
    same_batch = (mi // L) == (mj // L)
    tri = jnp.where(jnp.logical_and(same_batch, mj <= mi), 1.0, 0.0).astype(BF16)
    hi, mid, lo = _split3(logw)
    cl = _dot(tri, hi) + _dot(tri, mid) + _dot(tri, lo)
    e_neg = jnp.exp(-cl)
    a_f = (-kk) * jnp.exp(cl - logw)
    a_t = a_f.astype(BF16)
    b_t = ((kk * a) * e_neg).astype(BF16)
    k_t = (k * e_neg).astype(BF16)
    r_t = (r * jnp.exp(cl)).astype(BF16)
    v_b = v.astype(BF16)

    ti = lax.broadcasted_iota(jnp.int32, (L, L), 0)
    tj = lax.broadcasted_iota(jnp.int32, (L, L), 1)
    upper_strict = ti < tj
    lower_incl = tj <= ti
    eye = (ti == tj).astype(F32)
    n_double = int(math.log2(L)) - 1

    chains = []
    for b in range(nb):
        rows = slice(b * L, (b + 1) * L)
        a_tt = a_f[rows].T.astype(BF16)
        v_tt = v[rows].T.astype(BF16)
        p_end = jnp.exp(cl[(b + 1) * L - 1:(b + 1) * L, :])
        for h in range(N_HEADS):
            sl = slice(h * HEAD_DIM, (h + 1) * HEAD_DIM)
            chains.append(dict(b=b, h=h, rows=rows, sl=sl, a=a_t[rows, sl], bb=b_t[rows, sl], k=k_t[rows, sl],
                               r=r_t[rows, sl], v=v_b[rows, sl], att=a_tt[sl, :], vtt=v_tt[sl, :],
                               p_end=p_end[:, sl]))
    for c in chains:
        both = _dot_nt(jnp.concatenate([c["bb"], c["k"]], axis=0), c["a"])
        c["tab"] = jnp.where(upper_strict, both[:L], 0.0)
        c["tak"] = jnp.where(upper_strict, both[L:], 0.0).astype(BF16)
        c["trb"] = jnp.where(lower_incl, _dot_nt(c["r"], c["bb"]), 0.0).astype(BF16)
        c["trk"] = jnp.where(lower_incl, _dot_nt(c["r"], c["k"]), 0.0).astype(BF16)
    for c in chains:
        c["inv"] = eye + c["tab"]
        tb16 = c["tab"].astype(BF16)
        c["pw"] = _dot(tb16, tb16).astype(BF16)
    for step in range(n_double):
        for c in chains:
            if step < n_double - 1:
                prod = _dot(jnp.concatenate([c["inv"].astype(BF16), c["pw"]], axis=0), c["pw"])
                c["inv"] = c["inv"] + prod[:L]
                c["pw"] = prod[L:].astype(BF16)
            else:
                c["inv"] = c["inv"] + _dot(c["inv"].astype(BF16), c["pw"])
    for c in chains:
        c["st"] = st_ref[c["b"], c["h"]]
        c["stb"] = c["st"].astype(BF16)
        c["rhs"] = _dot(c["stb"], c["att"]) + _dot(c["vtt"], c["tak"])
    for c in chains:
        c["ut"] = _dot(c["rhs"].astype(BF16), c["inv"].astype(BF16)).astype(BF16)
    for c in chains:
        st_ref[c["b"], c["h"]] = (c["st"] + _dot(c["ut"], c["bb"]) + _dot(c["vtt"], c["k"])) * c["p_end"]
    for c in chains:
        ysc_ref[c["rows"], c["sl"]] = (_dot_nt(c["r"], c["stb"]) + _dot_nt(c["trb"], c["ut"])
                                       + _dot(c["trk"], c["v"]))

    y = ysc_ref[...]
    mean = _dot_sel(y, hsum) * (1.0 / HEAD_DIM)
    dev = y - mean
    var = _dot_sel(dev * dev, hsum) * (1.0 / HEAD_DIM)
    y = dev * lax.rsqrt(var + RWKV_GN_EPS) * row(V_GNW) + row(V_GNB)
    y_ref[...] = ((y + bonus) * g).reshape(nb, L, w)


def _rwkv(rkv, lora, v_first, vec, w2, a2, g2, v2, hsum):
    b, s, _ = rkv.shape
    L = RWKV_CHUNK
    has_vres = v_first is not None
    row = lambda wd: pl.BlockSpec((b, L, wd), lambda i: (0, i, 0))
    ins = [rkv, lora] + ([v_first] if has_vres else []) + [vec, w2, a2, g2] + ([v2] if has_vres else []) + [hsum]
    in_specs = [row(3 * WIDTH), row(2 * LORA_HALF)] + ([row(WIDTH)] if has_vres else [])
    in_specs += [_const_spec(c.shape) for c in ins[len(in_specs):]]
    y_shape = jax.ShapeDtypeStruct((b, s, WIDTH), F32)
    out = pl.pallas_call(
        functools.partial(_rwkv_body, has_vres),
        grid=(s // L,),
        in_specs=in_specs,
        out_specs=row(WIDTH) if has_vres else [row(WIDTH), row(WIDTH)],
        out_shape=y_shape if has_vres else [y_shape, y_shape],
        scratch_shapes=[pltpu.VMEM((b, N_HEADS, HEAD_DIM, HEAD_DIM), F32),
                        pltpu.VMEM((b, 1, 3 * WIDTH + LORA_HALF), F32),
                        pltpu.VMEM((b * L, WIDTH), F32)],
        compiler_params=_params(("arbitrary",)),
        name="rwkv_l1" if has_vres else "rwkv_l0",
    )(*ins)
    return (out, v_first) if has_vres else (out[0], out[1])


def _lru_body(x_ref, cw_ref, cb_ref, wa_ref, ba_ref, wx_ref, bx_ref, lam_ref, o_ref, a_ref, carry_ref, h_ref):
    tb = x_ref.shape[0]

    @pl.when(pl.program_id(1) == 0)
    def _():
        carry_ref[...] = jnp.zeros_like(carry_ref)
        h_ref[...] = jnp.zeros_like(h_ref)

    x = x_ref[...]
    full = jnp.concatenate([carry_ref[...], x], axis=0)
    carry_ref[...] = x[tb - SUBLANES:]
    cw = cw_ref[...]
    taps = cw.shape[0]
    xc = cb_ref[...] + cw[taps - 1:taps, :] * x
    for j in range(1, taps):
        xc = xc + cw[taps - 1 - j:taps - j, :] * pltpu.roll(full, j, 0)[SUBLANES:]
    xcb = xc.astype(BF16)
    r = _sigmoid(_dot(xcb, wa_ref[...]) + ba_ref[...])
    i = _sigmoid(_dot(xcb, wx_ref[...]) + bx_ref[...])
    log_a = -LRU_C * r * _softplus(-lam_ref[...])
    a = jnp.exp(log_a)
    bb = jnp.sqrt(-jnp.tanh(log_a) * (a * a + 1.0)) * (i * xc)
    in_tile = lax.broadcasted_iota(jnp.int32, (tb, 1), 0) % SUBLANES
    for k in (1, 2, 4):
        keep = in_tile >= k
        bb = jnp.where(keep, a * pltpu.roll(bb, k, 0) + bb, bb)
        a = jnp.where(keep, a * pltpu.roll(a, k, 0), a)
    a_ref[...] = a
    o_ref[...] = bb

    def tile(n, h):
        r0 = pl.multiple_of(n * SUBLANES, SUBLANES)
        res = a_ref[pl.ds(r0, SUBLANES), :] * h + o_ref[pl.ds(r0, SUBLANES), :]
        o_ref[pl.ds(r0, SUBLANES), :] = res
        return res[SUBLANES - 1:SUBLANES, :]

    h_ref[0:1, :] = lax.fori_loop(0, tb // SUBLANES, tile, h_ref[0:1, :], unroll=8)


def _lru(x, cw, cb, wa, ba, wx, bx, lam, tb=512):
    b, s, w = x.shape
    row = pl.BlockSpec((None, tb, w), lambda bi, i: (bi, i, 0))
    consts = (cw, cb, wa, ba, wx, bx, lam)
    return pl.pallas_call(
        _lru_body,
        grid=(b, s // tb),
        in_specs=[row] + [_const_spec(c.shape) for c in consts],
        out_specs=row,
        out_shape=jax.ShapeDtypeStruct((b, s, w), F32),
        scratch_shapes=[pltpu.VMEM((tb, w), F32), pltpu.VMEM((SUBLANES, w), F32), pltpu.VMEM((SUBLANES, w), F32)],
        compiler_params=_params(("parallel", "arbitrary")),
        name="lru",
    )(x, *consts)


def _from_residue_view(ref, dil, unperm):
    if dil == 1:
        return ref[...].astype(F32)
    width = ref.shape[1] // dil
    stacked = jnp.concatenate([ref[:, r * width:(r + 1) * width] for r in range(dil)], axis=0)
    if stacked.dtype == BF16:
        return _dot(unperm, stacked)
    hi, lo = _split2(stacked)
    return _dot(unperm, hi) + _dot(unperm, lo)


def _merge_body(x_ref, h_ref, o1_ref, o2_ref, o3_ref, l1_ref, l2_ref, l3_ref, yb_ref, yc_ref, yd_ref,
                wg_ref, wb_ref, wo_ref, exp_ref, up2_ref, up3_ref, out_ref):
    hb = h_ref[...]
    unperms = (None, up2_ref[...], up3_ref[...])
    o1, o2, o3 = (_from_residue_view(r, d, p) for r, d, p in zip((o1_ref, o2_ref, o3_ref), DILATIONS, unperms))
    l1, l2, l3 = (_from_residue_view(r, d, p) for r, d, p in zip((l1_ref, l2_ref, l3_ref), DILATIONS, unperms))
    m = jnp.maximum(jnp.maximum(l1, l2), l3)
    e1, e2, e3 = jnp.exp(l1 - m), jnp.exp(l2 - m), jnp.exp(l3 - m)
    inv = 1.0 / (e1 + e2 + e3)
    expand = exp_ref[...]
    ya = (_dot_sel(e1 * inv, expand) * o1 + _dot_sel(e2 * inv, expand) * o2 + _dot_sel(e3 * inv, expand) * o3)
    merged = None
    for n, y in enumerate((ya, yb_ref[...], yc_ref[...], yd_ref[...])):
        gate = _sigmoid(_dot(hb, wg_ref[:, n * D_MODEL:(n + 1) * D_MODEL]))
        term = gate * _dot(y.astype(BF16), wb_ref[n])
        merged = term if merged is None else merged + term
    out_ref[...] = x_ref[...] + _dot(merged.astype(BF16), wo_ref[...])


def _merge(x, h, os, ls, yb, yc, yd, wg, wb, wo, expand, tm=256):
    b, s, d = x.shape
    row = lambda wd: pl.BlockSpec((None, tm, wd), lambda bi, i: (bi, i, 0))
    res = lambda wd: [pl.BlockSpec((None, tm // dil, dil * wd), lambda bi, i: (bi, i, 0)) for dil in DILATIONS]
    consts = (wg, wb, wo, expand) + tuple(_dilated_perm(tm, dil).T for dil in DILATIONS[1:])
    return pl.pallas_call(
        _merge_body,
        grid=(b, s // tm),
        in_specs=[row(d), row(d)] + res(WIDTH) + res(LANES)
        + [_seg_view_spec(s, tm), row(WIDTH), row(WIDTH)] + [_const_spec(c.shape) for c in consts],
        out_specs=row(d),
        out_shape=jax.ShapeDtypeStruct((b, s, d), F32),
        compiler_params=_params(("parallel", "parallel")),
        name="merge",
    )(x, h, *os, *ls, yb, yc, yd, *consts)


def _ffn_body(final, x_ref, g_ref, wg_ref, wu_ref, wd_ref, gf_ref, out_ref):
    x = x_ref[...]
    hb = _rms(x, g_ref[...]).astype(BF16)
    gate = _dot(hb, wg_ref[...])
    up = _dot(hb, wu_ref[...])
    y = x + _dot((gate * _sigmoid(gate) * up).astype(BF16), wd_ref[...])
    out_ref[...] = _rms(y, gf_ref[...]) if final else y


def _ffn(x, g, wg, wu, wd, gf, final, tm=256):
    b, s, d = x.shape
    row = pl.BlockSpec((None, tm, d), lambda bi, i: (bi, i, 0))
    consts = (g, wg, wu, wd, gf)
    return pl.pallas_call(
        functools.partial(_ffn_body, final),
        grid=(b, s // tm),
        in_specs=[row] + [_const_spec(c.shape) for c in consts],
        out_specs=row,
        out_shape=jax.ShapeDtypeStruct((b, s, d), F32),
        compiler_params=_params(("parallel", "parallel")),
        name="ffn_final" if final else "ffn",
    )(x, *consts)


def _rope_tables(s):
    half = HEAD_DIM // 2
    inv = 1.0 / (ROPE_THETA ** (jnp.arange(0, HEAD_DIM, 2, dtype=F32) / HEAD_DIM))
    ang = jnp.arange(s, dtype=F32)[:, None] * inv[None, :]
    cos, sin = jnp.cos(ang), jnp.sin(ang)
    reps = LANES // HEAD_DIM
    cos_t = jnp.tile(jnp.concatenate([cos, cos], axis=-1), (1, reps))
    sin_t = jnp.tile(jnp.concatenate([-sin, sin], axis=-1), (1, reps))
    return cos_t, sin_t


def _block_diag(blocks):
    g, r, c = blocks.shape
    eye = jnp.eye(g, dtype=blocks.dtype)
    return (eye[:, None, :, None] * blocks[:, :, None, :]).reshape(g * r, g * c)


def _s5_params(lam_re, lam_im, log_dt, b_re, b_im, c_re, c_im):
    lr, li = lam_re.astype(F32), lam_im.astype(F32)
    dt = jnp.exp(log_dt.astype(F32))[:, None]
    mag = jnp.exp(lr * dt)
    ab_re, ab_im = mag * jnp.cos(li * dt), mag * jnp.sin(li * dt)
    nr = ab_re - 1.0
    den = lr * lr + li * li
    f_re = (nr * lr + ab_im * li) / den
    f_im = (ab_im * lr - nr * li) / den
    bb_re = f_re[..., None] * b_re - f_im[..., None] * b_im
    bb_im = f_re[..., None] * b_im + f_im[..., None] * b_re
    gh = S5_GROUPS // S5_HALVES
    swap = lambda t: jnp.transpose(t, (0, 2, 1))
    halves = lambda t: [t[h * gh:(h + 1) * gh] for h in range(S5_HALVES)]
    bbd = jnp.stack([jnp.concatenate([_block_diag(swap(re)), _block_diag(swap(im))], axis=-1)
                     for re, im in zip(halves(bb_re), halves(bb_im))]).astype(BF16)
    cbd = jnp.stack([jnp.concatenate([_block_diag(swap(re)), -_block_diag(swap(im))], axis=0)
                     for re, im in zip(halves(c_re), halves(c_im))]).astype(BF16)
    flat = lambda t: [h.reshape(1, -1) for h in halves(t)]
    cat = lambda parts: jnp.concatenate(parts, axis=-1)
    a_plain = cat([cat([re, im]) for re, im in zip(flat(ab_re), flat(ab_im))])
    a_same = cat([cat([re, re]) for re in flat(ab_re)])
    a_cross = cat([cat([-im, im]) for im in flat(ab_im)])
    rep = lambda t: jnp.broadcast_to(t, (S5_SEGS, t.shape[-1]))
    return bbd, rep(a_same), rep(a_cross), a_plain, cbd


def _lora_weights(mu_wag, w1, a1, g1, mu_v, v1):
    d = w1.shape[0]
    pad = lambda t: jnp.pad(t, ((0, 0), (0, LORA_HALF - t.shape[1])))
    v1 = jnp.zeros((d, 32), F32) if v1 is None else v1
    mu_v = jnp.zeros((d,), F32) if mu_v is None else mu_v
    mats = (w1, a1, g1, v1)
    mus = (mu_wag[0], mu_wag[1], mu_wag[2], mu_v)
    keep = pad(jnp.concatenate([m * (1.0 - mu)[:, None] for m, mu in zip(mats, mus)], axis=1))
    shifted = pad(jnp.concatenate([m * mu[:, None] for m, mu in zip(mats, mus)], axis=1))
    return jnp.concatenate([keep, shifted], axis=1).astype(BF16)


def kernel(x, norm_mix, w_in, s5_lam_re, s5_lam_im, s5_log_dt, s5_b_re, s5_b_im, s5_c_re, s5_c_im, s5_d, s5_w_glu, s5_b_glu, rw_mu_rkv, rw_mu_wag, rw_w0, rw_w1, rw_w2, rw_a0, rw_a1, rw_a2, rw_g1, rw_g2, rw_k_k, rw_k_a, rw_r_k, rw_gn_w, rw_gn_b, rw_mu_v, rw_v0, rw_v1, rw_v2, lru_conv_w, lru_conv_b, lru_w_a, lru_b_a, lru_w_x, lru_b_x, lru_lam, w_branch, w_out, norm_ffn, w_ffn_gate, w_ffn_up, w_ffn_down, norm_final):
    b, s, d = x.shape
    depth = w_in.shape[0]
    w = WIDTH
    cos_t, sin_t = _rope_tables(s)
    head_id = jnp.arange(w) // HEAD_DIM
    hsum = (head_id[:, None] == head_id[None, :]).astype(BF16)
    expand = (jnp.arange(LANES)[:, None] == head_id[None, :]).astype(BF16)
    bf = lambda t: t.astype(BF16)
    row = lambda t: t.reshape(1, -1).astype(F32)

    v_first = None
    for l in range(depth):
        wl = w_in[l]
        cols = np.cumsum((0, w, w, w, w, w, w, w, w))
        seg = lambda i, j: bf(wl[:, cols[i]:cols[j]])
        has_vres = l > 0
        wlora = _lora_weights(rw_mu_wag[l], rw_w1[l], rw_a1[l], rw_g1[l],
                              rw_mu_v[l - 1] if has_vres else None, rw_v1[l - 1] if has_vres else None)
        h, u, rkv, xlru, lora, *qkv_views = _inproj(
            x, row(norm_mix[l]), cos_t, sin_t, seg(0, 2), seg(2, 3), seg(3, 4), seg(4, 7), seg(7, 8), wlora)

        att = [_attention(view, dil) for view, dil in zip(qkv_views, DILATIONS)]

        s5p = _s5_params(s5_lam_re[l], s5_lam_im[l], s5_log_dt[l], s5_b_re[l], s5_b_im[l], s5_c_re[l], s5_c_im[l])
        y_b = _s5(u, *s5p, row(s5_d[l]), bf(s5_w_glu[l]), row(s5_b_glu[l]))

        zero = jnp.zeros((w,), F32)
        vec = jnp.stack([rw_mu_rkv[l, 0], rw_mu_rkv[l, 1], rw_mu_rkv[l, 2], rw_w0[l], rw_a0[l],
                         rw_v0[l - 1] if has_vres else zero, rw_k_k[l], rw_k_a[l], rw_r_k[l].reshape(-1),
                         rw_gn_w[l], rw_gn_b[l]] + [zero] * 5).astype(F32)
        v2 = bf(jnp.pad(rw_v2[l - 1], ((0, 32), (0, 0)))) if has_vres else None
        y_c, v_c = _rwkv(rkv, lora, v_first, vec, bf(rw_w2[l]), bf(rw_a2[l]), bf(rw_g2[l]), v2, hsum)
        if l == 0:
            v_first = v_c

        y_d = _lru(xlru, lru_conv_w[l].astype(F32), row(lru_conv_b[l]), bf(_block_diag(lru_w_a[l])), row(lru_b_a[l]),
                   bf(_block_diag(lru_w_x[l])), row(lru_b_x[l]), row(lru_lam[l]))

        x = _merge(x, h, [o for o, _ in att], [ls for _, ls in att], y_b, y_c, y_d,
                   bf(wl[:, cols[8]:]), bf(w_branch[l]), bf(w_out[l]), expand)
        x = _ffn(x, row(norm_ffn[l]), bf(w_ffn_gate[l]), bf(w_ffn_up[l]), bf(w_ffn_down[l]), row(norm_final),
                 final=(l == depth - 1))
    return x
```

```python
import functools
import math

import jax
import jax.numpy as jnp
import numpy as np
from jax import lax
from jax.experimental import pallas as pl
from jax.experimental.pallas import tpu as pltpu

F32 = jnp.float32
BF16 = jnp.bfloat16

D_MODEL = 1024
N_HEADS = 8
HEAD_DIM = 64
WIDTH = N_HEADS * HEAD_DIM
DILATIONS = (1, 4, 16)
ATT_BLOCK = 128
ROPE_THETA = 10000.0
S5_GROUPS = 32
S5_GROUP = 16
S5_STATE = 64
S5_NSTATE = S5_GROUPS * S5_STATE
RWKV_GN_EPS = 64e-5
RWKV_CHUNK = 64
LORA_HALF = 384
LRU_BLOCKS = 8
LRU_C = 8.0
D_FF = 2816
RMS_EPS = 1e-6
NEG_BIG = -1e30

LANES = 128
SUBLANES = 8
S5_SEGS = SUBLANES
S5_HALVES = 2
S5_HALF_STATES = S5_NSTATE // S5_HALVES
VMEM_LIMIT = 56 * 1024 * 1024

NT_DIMS = (((1,), (1,)), ((), ()))


def _dot(a, b):
    return jnp.dot(a, b, preferred_element_type=F32)


def _dot_nt(a, b):
    return lax.dot_general(a, b, NT_DIMS, preferred_element_type=F32)


def _split2(x):
    hi = x.astype(BF16)
    lo = (x - hi.astype(F32)).astype(BF16)
    return hi, lo


def _split3(x):
    hi = x.astype(BF16)
    r1 = x - hi.astype(F32)
    mid = r1.astype(BF16)
    lo = (r1 - mid.astype(F32)).astype(BF16)
    return hi, mid, lo


def _dot_sel(x, sel):
    hi, lo = _split2(x)
    return _dot(hi, sel) + _dot(lo, sel)


def _head_sums(xs, hsum):
    m = xs[0].shape[0]
    parts = []
    for x in xs:
        parts += list(_split2(x))
    stacked = jnp.concatenate(parts, axis=0)
    half = hsum.shape[0] // 2
    block = hsum[:half, :half]
    out = jnp.concatenate([_dot(stacked[:, :half], block), _dot(stacked[:, half:], block)], axis=-1)
    return [out[2 * j * m:(2 * j + 1) * m] + out[(2 * j + 1) * m:(2 * j + 2) * m] for j in range(len(xs))]


def _sigmoid(x):
    return 1.0 / (1.0 + jnp.exp(-x))


def _softplus(z):
    return jnp.maximum(z, 0.0) + jnp.log1p(jnp.exp(-jnp.abs(z)))


def _rms(x, g):
    ms = jnp.mean(x * x, axis=-1, keepdims=True)
    return x * lax.rsqrt(ms + RMS_EPS) * g


def _const_spec(shape):
    n = len(shape)
    return pl.BlockSpec(shape, lambda *_: (0,) * n, pipeline_mode=pl.Buffered(1))


def _params(sem):
    return pltpu.CompilerParams(dimension_semantics=sem, vmem_limit_bytes=VMEM_LIMIT)


def _seg_view_spec(s, tm):
    tiles_per_seg = s // S5_SEGS // tm
    return pl.BlockSpec((None, tm, WIDTH), lambda bi, i: (bi, i % tiles_per_seg, i // tiles_per_seg))


def _dilated_perm(tm, dil):
    dst = jnp.arange(tm)
    src = (dst % (tm // dil)) * dil + dst // (tm // dil)
    return (src[:, None] == jnp.arange(tm)[None, :]).astype(BF16)


def _lru_apply(x, cw_ref, cb_ref, wa_ref, ba_ref, wx_ref, bx_ref, lam_ref, o_ref, a_ref, carry_ref, h_ref):
    tb = x.shape[0]
    full = jnp.concatenate([carry_ref[...], x], axis=0)
    carry_ref[...] = x[tb - SUBLANES:]
    cw = cw_ref[...]
    taps = cw.shape[0]
    xc = cb_ref[...] + cw[taps - 1:taps, :] * x
    for j in range(1, taps):
        xc = xc + cw[taps - 1 - j:taps - j, :] * pltpu.roll(full, j, 0)[SUBLANES:]
    xcb = xc.astype(BF16)
    r = _sigmoid(_dot(xcb, wa_ref[...]) + ba_ref[...])
    i = _sigmoid(_dot(xcb, wx_ref[...]) + bx_ref[...])
    log_a = -LRU_C * r * _softplus(-lam_ref[...])
    a = jnp.exp(log_a)
    bb = jnp.sqrt(-jnp.tanh(log_a) * (a * a + 1.0)) * (i * xc)
    in_tile = lax.broadcasted_iota(jnp.int32, (tb, 1), 0) % SUBLANES
    for k in (1, 2, 4):
        keep = in_tile >= k
        bb = jnp.where(keep, a * pltpu.roll(bb, k, 0) + bb, bb)
        a = jnp.where(keep, a * pltpu.roll(a, k, 0), a)
    a_ref[...] = a
    o_ref[...] = bb
    h = h_ref[0:1, :]
    for n in range(tb // SUBLANES):
        rows = slice(n * SUBLANES, (n + 1) * SUBLANES)
        res = a_ref[rows, :] * h + o_ref[rows, :]
        o_ref[rows, :] = res
        h = res[SUBLANES - 1:SUBLANES, :]
    h_ref[0:1, :] = h


def _inproj_body(x_ref, g_ref, cos_ref, sin_ref, wqk_ref, wv_ref, wu_ref, wrkv_ref, wlru_ref, wlora_ref, *rest):
    n_dil = len(DILATIONS) - 1
    lru_consts, rest = rest[:7], rest[7:]
    perm_refs, (h_ref, u_ref, rkv_ref, lru_ref, lora_ref, qkv_ref), rest = rest[:n_dil], rest[n_dil:n_dil + 6], rest[n_dil + 6:]
    dil_refs, lru_scratch = rest[:n_dil], rest[n_dil:]
    tm = x_ref.shape[0]

    @pl.when(pl.program_id(1) == 0)
    def _():
        for ref in lru_scratch[1:]:
            ref[...] = jnp.zeros_like(ref)

    hb = _rms(x_ref[...], g_ref[...]).astype(BF16)
    h_ref[...] = hb
    _lru_apply(_dot(hb, wlru_ref[...]), *lru_consts, lru_ref, *lru_scratch)
    qk = _dot(hb, wqk_ref[...])
    reps = 2 * WIDTH // LANES
    cos = jnp.concatenate([cos_ref[...]] * reps, axis=-1)
    sin = jnp.concatenate([sin_ref[...]] * reps, axis=-1)
    lane = lax.broadcasted_iota(jnp.int32, qk.shape, 1)
    first_half = (lane % HEAD_DIM) < (HEAD_DIM // 2)
    partner = jnp.where(first_half,
                        pltpu.roll(qk, 2 * WIDTH - HEAD_DIM // 2, 1),
                        pltpu.roll(qk, HEAD_DIM // 2, 1))
    rot = qk * cos + partner * sin
    qkv = jnp.concatenate([(rot[:, :WIDTH] * (HEAD_DIM ** -0.5)).astype(BF16), rot[:, WIDTH:].astype(BF16),
                           _dot(hb, wv_ref[...]).astype(BF16)], axis=-1)
    qkv_ref[...] = qkv
    for dil, perm_ref, out_ref in zip(DILATIONS[1:], perm_refs, dil_refs):
        grouped = _dot(perm_ref[...], qkv).astype(BF16)
        rows = tm // dil
        for r in range(dil):
            out_ref[:, r * 3 * WIDTH:(r + 1) * 3 * WIDTH] = grouped[r * rows:(r + 1) * rows]
    u_ref[...] = _dot(hb, wu_ref[...])
    rkv_ref[...] = _dot(hb, wrkv_ref[...])
    lora_ref[...] = _dot(hb, wlora_ref[...])


def _inproj(x, g, cos, sin, wqk, wv, wu, wrkv, wlru, wlora, lru_consts, tm=256):
    b, s, d = x.shape
    grid = (b, s // tm)
    row = lambda w: pl.BlockSpec((None, tm, w), lambda bi, i: (bi, i, 0))
    tab = pl.BlockSpec((tm, LANES), lambda bi, i: (i, 0))
    widths = (d, WIDTH, 3 * WIDTH, WIDTH, 2 * LORA_HALF, 3 * WIDTH)
    dtypes = (BF16, F32, F32, F32, F32, BF16)
    out_specs = [row(w) for w in widths]
    out_shape = [jax.ShapeDtypeStruct((b, s, w), dt) for w, dt in zip(widths, dtypes)]
    out_specs[1] = _seg_view_spec(s, tm)
    out_shape[1] = jax.ShapeDtypeStruct((b, s // S5_SEGS, S5_SEGS * WIDTH), F32)
    for dil in DILATIONS[1:]:
        out_specs.append(pl.BlockSpec((None, tm // dil, dil * 3 * WIDTH), lambda bi, i: (bi, i, 0)))
        out_shape.append(jax.ShapeDtypeStruct((b, s // dil, dil * 3 * WIDTH), BF16))
    consts = (wqk, wv, wu, wrkv, wlru, wlora) + tuple(lru_consts) + tuple(_dilated_perm(tm, dil) for dil in DILATIONS[1:])
    return pl.pallas_call(
        _inproj_body,
        grid=grid,
        in_specs=[row(d), _const_spec((1, d)), tab, tab] + [_const_spec(c.shape) for c in consts],
        out_specs=out_specs,
        out_shape=out_shape,
        scratch_shapes=[pltpu.VMEM((tm, WIDTH), F32), pltpu.VMEM((SUBLANES, WIDTH), F32), pltpu.VMEM((SUBLANES, WIDTH), F32)],
        compiler_params=_params(("parallel", "arbitrary")),
        name="inproj",
    )(x, g, cos, sin, *consts)


def _attn_body(q_ref, kc_ref, kp_ref, vc_ref, o_ref, l_ref, vt_ref):
    i = pl.program_id(2)
    blk = ATT_BLOCK
    n_sub = q_ref.shape[0] // blk

    @pl.when(i == 0)
    def _():
        vt_ref[...] = jnp.zeros_like(vt_ref)

    rows = [slice(j * blk, (j + 1) * blk) for j in range(n_sub)]
    vts = [vt_ref[...]] + [vc_ref[r, :].astype(F32).T.astype(BF16) for r in rows]
    vt_ref[...] = vts[-1]
    key = lax.broadcasted_iota(jnp.int32, (2 * blk, blk), 0)
    qry = lax.broadcasted_iota(jnp.int32, (2 * blk, blk), 1)
    in_cur = jnp.logical_and(key >= blk, key - blk <= qry)
    in_prev = jnp.logical_and(key < blk, key >= qry)
    masks = [jnp.logical_or(in_cur, jnp.logical_and(in_prev, i > 0))] + [jnp.logical_or(in_cur, in_prev)] * (n_sub - 1)
    heads = [slice(h * HEAD_DIM, (h + 1) * HEAD_DIM) for h in range(N_HEADS)]
    pairs = [(j, sl) for j in range(n_sub) for sl in heads]
    prev_keys = lambda j, sl: kp_ref[:, sl] if j == 0 else kc_ref[rows[j - 1], sl]
    keys = [jnp.concatenate([prev_keys(j, sl), kc_ref[rows[j], sl]], axis=0) for j, sl in pairs]
    st = [jnp.where(masks[j], _dot_nt(k, q_ref[rows[j], sl]), NEG_BIG) for k, (j, sl) in zip(keys, pairs)]
    ms = [jnp.max(s, axis=0, keepdims=True) for s in st]
    ps = [jnp.exp(s - m) for s, m in zip(st, ms)]
    dens = [jnp.sum(p, axis=0, keepdims=True) for p in ps]
    outs = []
    for n, (j, sl) in enumerate(pairs):
        vt = jnp.concatenate([vts[j][sl, :], vts[j + 1][sl, :]], axis=1)
        outs.append(_dot(vt, ps[n].astype(BF16)) * (1.0 / dens[n]))
    pad = [jnp.zeros((LANES - N_HEADS, blk), F32)]
    for j in range(n_sub):
        mine = slice(j * N_HEADS, (j + 1) * N_HEADS)
        o_ref[rows[j], :] = jnp.concatenate(outs[mine], axis=0).T.astype(BF16)
        lse = jnp.concatenate([m + jnp.log(d) for m, d in zip(ms[mine], dens[mine])] + pad, axis=0)
        l_ref[rows[j], :] = lse.T


def _attention(qkv_view, dil):
    b, sub, _ = qkv_view.shape
    w = WIDTH
    n_blocks = sub // ATT_BLOCK
    n_sub = 2 if n_blocks % 2 == 0 else 1
    step = n_sub * ATT_BLOCK
    cur = lambda j: pl.BlockSpec((None, step, w), lambda bi, r, i: (bi, i, 3 * r + j))
    prev_k = pl.BlockSpec((None, ATT_BLOCK, w), lambda bi, r, i: (bi, jnp.maximum(i * n_sub - 1, 0), 3 * r + 1))
    return pl.pallas_call(
        _attn_body,
        grid=(b, dil, n_blocks // n_sub),
        in_specs=[cur(0), cur(1), prev_k, cur(2)],
        out_specs=[pl.BlockSpec((None, step, w), lambda bi, r, i: (bi, i, r)),
                   pl.BlockSpec((None, step, LANES), lambda bi, r, i: (bi, i, r))],
        out_shape=[jax.ShapeDtypeStruct((b, sub, dil * w), BF16),
                   jax.ShapeDtypeStruct((b, sub, dil * LANES), F32)],
        scratch_shapes=[pltpu.VMEM((w, ATT_BLOCK), BF16)],
        compiler_params=_params(("parallel", "parallel", "arbitrary")),
        name=f"attn_d{dil}",
    )(*([qkv_view] * 4))


def _gelu_tanh(x):
    return 0.5 * x * (1.0 + jnp.tanh(math.sqrt(2.0 / math.pi) * (x + 0.044715 * (x * x * x))))


def _s5_slab(u_ref, perm_ref):
    u_stack = jnp.concatenate([u_ref[:, sg * WIDTH:(sg + 1) * WIDTH] for sg in range(S5_SEGS)], axis=0)
    u_slab = _dot(perm_ref[...], u_stack.astype(BF16)).astype(BF16)
    return u_stack, u_slab


def _s5_scan(u_slab, bbd_ref, asame_ref, across_ref, bu_ref, st_ref, store):
    hs = S5_HALF_STATES
    cin = WIDTH // S5_HALVES
    cols = [slice(half * 2 * hs, (half + 1) * 2 * hs) for half in range(S5_HALVES)]
    for half in range(S5_HALVES):
        bu_ref[half] = _dot(u_slab[:, half * cin:(half + 1) * cin], bbd_ref[half])
    xs = [st_ref[:, c] for c in cols]
    for t in range(bu_ref.shape[1] // S5_SEGS):
        rows = slice(t * S5_SEGS, (t + 1) * S5_SEGS)
        for half, c in enumerate(cols):
            x = xs[half]
            swapped = jnp.concatenate([x[:, hs:], x[:, :hs]], axis=-1)
            x = asame_ref[:, c] * x + across_ref[:, c] * swapped + bu_ref[half, rows, :]
            if store:
                bu_ref[half, rows, :] = x
            xs[half] = x
    for half, c in enumerate(cols):
        st_ref[:, c] = xs[half]


def _s5_state_body(u_ref, perm_ref, bbd_ref, asame_ref, across_ref, end_ref, bu_ref, st_ref):
    @pl.when(pl.program_id(1) == 0)
    def _():
        st_ref[...] = jnp.zeros_like(st_ref)

    _, u_slab = _s5_slab(u_ref, perm_ref)
    _s5_scan(u_slab, bbd_ref, asame_ref, across_ref, bu_ref, st_ref, store=False)

    @pl.when(pl.program_id(1) == pl.num_programs(1) - 1)
    def _():
        end_ref[...] = st_ref[...]


def _cmul(z, w):
    hs = z.shape[-1] // 2
    zr, zi, wr, wi = z[:, :hs], z[:, hs:], w[:, :hs], w[:, hs:]
    return jnp.concatenate([zr * wr - zi * wi, zr * wi + zi * wr], axis=-1)


def _s5_out_body(seg_len, u_ref, end_ref, perm_ref, permt_ref, bbd_ref, asame_ref, across_ref, aplain_ref,
                 cbd_ref, d_ref, wglu_ref, bglu_ref, o_ref, bu_ref, st_ref):
    hs = S5_HALF_STATES
    tt = u_ref.shape[0]

    @pl.when(pl.program_id(1) == 0)
    def _():
        for half in range(S5_HALVES):
            cols = slice(half * 2 * hs, (half + 1) * 2 * hs)
            base = aplain_ref[:, cols]
            power = None
            e = seg_len
            while e:
                if e & 1:
                    power = base if power is None else _cmul(power, base)
                e >>= 1
                if e:
                    base = _cmul(base, base)
            x = jnp.zeros((1, 2 * hs), F32)
            for sg in range(S5_SEGS):
                st_ref[sg:sg + 1, cols] = x
                x = _cmul(x, power) + end_ref[sg:sg + 1, cols]

    u_stack, u_slab = _s5_slab(u_ref, perm_ref)
    _s5_scan(u_slab, bbd_ref, asame_ref, across_ref, bu_ref, st_ref, store=True)
    ys = [_dot(bu_ref[half].astype(BF16), cbd_ref[half]) for half in range(S5_HALVES)]
    hi, lo = _split2(jnp.concatenate(ys, axis=-1))
    y = _dot(permt_ref[...], hi) + _dot(permt_ref[...], lo) + d_ref[...] * u_stack
    z = _dot(_gelu_tanh(y).astype(BF16), wglu_ref[...]) + bglu_ref[...]
    out = z[:, :WIDTH] * _sigmoid(z[:, WIDTH:])
    for sg in range(S5_SEGS):
        o_ref[:, sg * WIDTH:(sg + 1) * WIDTH] = out[sg * tt:(sg + 1) * tt]


def _s5(u_view, bbd, a_same, a_cross, a_plain, cbd, d_skip, wglu, bglu, tt=64):
    b, seg_len, wide = u_view.shape
    m = tt * S5_SEGS
    dst = jnp.arange(m)
    src = (dst % S5_SEGS) * tt + dst // S5_SEGS
    perm = (src[:, None] == jnp.arange(m)[None, :]).astype(BF16)
    row = pl.BlockSpec((None, tt, wide), lambda bi, i: (bi, i, 0))
    ends = pl.BlockSpec((None, S5_SEGS, 2 * S5_NSTATE), lambda bi, i: (bi, 0, 0))
    scratch = [pltpu.VMEM((S5_HALVES, m, 2 * S5_HALF_STATES), F32), pltpu.VMEM((S5_SEGS, 2 * S5_NSTATE), F32)]
    consts1 = (perm, bbd, a_same, a_cross)
    seg_end = pl.pallas_call(
        _s5_state_body,
        grid=(b, seg_len // tt),
        in_specs=[row] + [_const_spec(c.shape) for c in consts1],
        out_specs=ends,
        out_shape=jax.ShapeDtypeStruct((b, S5_SEGS, 2 * S5_NSTATE), F32),
        scratch_shapes=scratch,
        compiler_params=_params(("parallel", "arbitrary")),
        name="s5_state",
    )(u_view, *consts1)
    consts2 = (perm, perm.T, bbd, a_same, a_cross, a_plain, cbd, d_skip, wglu, bglu)
    return pl.pallas_call(
        functools.partial(_s5_out_body, seg_len),
        grid=(b, seg_len // tt),
        in_specs=[row, ends] + [_const_spec(c.shape) for c in consts2],
        out_specs=row,
        out_shape=jax.ShapeDtypeStruct((b, seg_len, wide), F32),
        scratch_shapes=scratch,
        compiler_params=_params(("parallel", "arbitrary")),
        name="s5_out",
    )(u_view, seg_end, *consts2)


V_MU_R, V_MU_K, V_MU_V, V_W0, V_A0, V_V0, V_KK, V_KA, V_RK, V_GNW, V_GNB = range(11)


def _rwkv_body(has_vres, *refs):
    if has_vres:
        (rkv_ref, lora_ref, vf_ref, vec_ref, w2_ref, a2_ref, g2_ref, v2_ref, hsum_ref,
         y_ref, st_ref, carry_ref, ysc_ref) = refs
    else:
        (rkv_ref, lora_ref, vec_ref, w2_ref, a2_ref, g2_ref, hsum_ref,
         y_ref, vout_ref, st_ref, carry_ref, ysc_ref) = refs
    nb, L = rkv_ref.shape[0], rkv_ref.shape[1]
    m = nb * L
    w = WIDTH

    @pl.when(pl.program_id(0) == 0)
    def _():
        st_ref[...] = jnp.zeros_like(st_ref)
        carry_ref[...] = jnp.zeros_like(carry_ref)

    vec = vec_ref[...]
    row = lambda j: vec[j:j + 1, :]
    hsum = hsum_ref[...]

    rkv = rkv_ref[...].reshape(m, 3 * w)
    lora = lora_ref[...].reshape(m, 2 * LORA_HALF)
    cat = jnp.concatenate([rkv, lora[:, LORA_HALF:]], axis=-1)
    row_id = lax.broadcasted_iota(jnp.int32, (m, 1), 0)
    prev = pltpu.roll(cat, 1, 0)
    for b in range(nb):
        prev = jnp.where(row_id == b * L, carry_ref[b], prev)
        carry_ref[b] = cat[(b + 1) * L - 1:(b + 1) * L, :]
    lerp = lambda j, mu: rkv[:, j * w:(j + 1) * w] + (prev[:, j * w:(j + 1) * w] - rkv[:, j * w:(j + 1) * w]) * mu
    r = lerp(0, row(V_MU_R))
    k = lerp(1, row(V_MU_K))
    v = lerp(2, row(V_MU_V))
    pre = lora[:, :LORA_HALF] + prev[:, 3 * w:]

    wx = row(V_W0) + _dot(jnp.tanh(pre[:, 0:64]).astype(BF16), w2_ref[...])
    logw = -jnp.exp(-_softplus(-wx) - 0.5)
    a = _sigmoid(row(V_A0) + _dot(pre[:, 64:128].astype(BF16), a2_ref[...]))
    g = _dot(_sigmoid(pre[:, 128:256]).astype(BF16), g2_ref[...])
    if has_vres:
        vf = vf_ref[...].reshape(m, w)
        v = v + (vf - v) * _sigmoid(row(V_V0) + _dot(pre[:, 256:320].astype(BF16), v2_ref[...]))
    else:
        vout_ref[...] = v.reshape(nb, L, w)

    kk = k * row(V_KK)
    k = k * (1.0 + (a - 1.0) * row(V_KA))
    kk_sq, rk_sum = _head_sums([kk * kk, r * k * row(V_RK)], hsum)
    kk = kk * lax.rsqrt(jnp.maximum(kk_sq, 1e-24))
    bonus = rk_sum * v

    mi = lax.broadcasted_iota(jnp.int32, (m, m), 0)
    mj = lax.broadcasted_iota(jnp.int32, (m, m), 1)
    same_batch = (mi // L) == (mj // L)
    tri = jnp.where(jnp.logical_and(same_batch, mj <= mi), 1.0, 0.0).astype(BF16)
    hi, mid, lo = _split3(logw)
    cl = _dot(tri, hi) + _dot(tri, mid) + _dot(tri, lo)
    e_neg = jnp.exp(-cl)
    a_f = (-kk) * jnp.exp(cl - logw)
    a_t = a_f.astype(BF16)
    b_t = ((kk * a) * e_neg).astype(BF16)
    k_t = (k * e_neg).astype(BF16)
    r_t = (r * jnp.exp(cl)).astype(BF16)
    v_b = v.astype(BF16)

    ti = lax.broadcasted_iota(jnp.int32, (L, L), 0)
    tj = lax.broadcasted_iota(jnp.int32, (L, L), 1)
    upper_strict = ti < tj
    lower_incl = tj <= ti
    eye = (ti == tj).astype(F32)
    n_double = int(math.log2(L)) - 1

    chains = []
    for b in range(nb):
        rows = slice(b * L, (b + 1) * L)
        a_tt = a_f[rows].T.astype(BF16)
        v_tt = v[rows].T.astype(BF16)
        p_end = jnp.exp(cl[(b + 1) * L - 1:(b + 1) * L, :])
        for h in range(N_HEADS):
            sl = slice(h * HEAD_DIM, (h + 1) * HEAD_DIM)
            chains.append(dict(b=b, h=h, rows=rows, sl=sl, a=a_t[rows, sl], bb=b_t[rows, sl], k=k_t[rows, sl],
                               r=r_t[rows, sl], v=v_b[rows, sl], att=a_tt[sl, :], vtt=v_tt[sl, :],
                               p_end=p_end[:, sl]))
    for c in chains:
        both = _dot_nt(jnp.concatenate([c["bb"], c["k"]], axis=0), c["a"])
        c["tab"] = jnp.where(upper_strict, both[:L], 0.0)
        c["tak"] = jnp.where(upper_strict, both[L:], 0.0).astype(BF16)
        c["trb"] = jnp.where(lower_incl, _dot_nt(c["r"], c["bb"]), 0.0).astype(BF16)
        c["trk"] = jnp.where(lower_incl, _dot_nt(c["r"], c["k"]), 0.0).astype(BF16)
    for c in chains:
        c["inv"] = eye + c["tab"]
        tb16 = c["tab"].astype(BF16)
        c["pw"] = _dot(tb16, tb16).astype(BF16)
    for step in range(n_double):
        for c in chains:
            if step < n_double - 1:
                prod = _dot(jnp.concatenate([c["inv"].astype(BF16), c["pw"]], axis=0), c["pw"])
                c["inv"] = c["inv"] + prod[:L]
                c["pw"] = prod[L:].astype(BF16)
            else:
                c["inv"] = c["inv"] + _dot(c["inv"].astype(BF16), c["pw"])
    for c in chains:
        c["st"] = st_ref[c["b"], c["h"]]
        c["stb"] = c["st"].astype(BF16)
        c["rhs"] = _dot(c["stb"], c["att"]) + _dot(c["vtt"], c["tak"])
    for c in chains:
        c["ut"] = _dot(c["rhs"].astype(BF16), c["inv"].astype(BF16)).astype(BF16)
    for c in chains:
        st_ref[c["b"], c["h"]] = (c["st"] + _dot(c["ut"], c["bb"]) + _dot(c["vtt"], c["k"])) * c["p_end"]
    for c in chains:
        ysc_ref[c["rows"], c["sl"]] = (_dot_nt(c["r"], c["stb"]) + _dot_nt(c["trb"], c["ut"])
                                       + _dot(c["trk"], c["v"]))

    y = ysc_ref[...]
    mean = _head_sums([y], hsum)[0] * (1.0 / HEAD_DIM)
    dev = y - mean
    var = _head_sums([dev * dev], hsum)[0] * (1.0 / HEAD_DIM)
    y = dev * lax.rsqrt(var + RWKV_GN_EPS) * row(V_GNW) + row(V_GNB)
    y_ref[...] = ((y + bonus) * g).reshape(nb, L, w)


def _rwkv(rkv, lora, v_first, vec, w2, a2, g2, v2, hsum):
    b, s, _ = rkv.shape
    L = RWKV_CHUNK
    has_vres = v_first is not None
    row = lambda wd: pl.BlockSpec((b, L, wd), lambda i: (0, i, 0))
    ins = [rkv, lora] + ([v_first] if has_vres else []) + [vec, w2, a2, g2] + ([v2] if has_vres else []) + [hsum]
    in_specs = [row(3 * WIDTH), row(2 * LORA_HALF)] + ([row(WIDTH)] if has_vres else [])
    in_specs += [_const_spec(c.shape) for c in ins[len(in_specs):]]
    y_shape = jax.ShapeDtypeStruct((b, s, WIDTH), F32)
    out = pl.pallas_call(
        functools.partial(_rwkv_body, has_vres),
        grid=(s // L,),
        in_specs=in_specs,
        out_specs=row(WIDTH) if has_vres else [row(WIDTH), row(WIDTH)],
        out_shape=y_shape if has_vres else [y_shape, y_shape],
        scratch_shapes=[pltpu.VMEM((b, N_HEADS, HEAD_DIM, HEAD_DIM), F32),
                        pltpu.VMEM((b, 1, 3 * WIDTH + LORA_HALF), F32),
                        pltpu.VMEM((b * L, WIDTH), F32)],
        compiler_params=_params(("arbitrary",)),
        name="rwkv_l1" if has_vres else "rwkv_l0",
    )(*ins)
    return (out, v_first) if has_vres else (out[0], out[1])


def _from_residue_view(ref, dil, unperm):
    if dil == 1:
        return ref[...].astype(F32)
    width = ref.shape[1] // dil
    stacked = jnp.concatenate([ref[:, r * width:(r + 1) * width] for r in range(dil)], axis=0)
    if stacked.dtype == BF16:
        return _dot(unperm, stacked)
    hi, lo = _split2(stacked)
    return _dot(unperm, hi) + _dot(unperm, lo)


def _merge_ffn_body(final, x_ref, h_ref, o1_ref, o2_ref, o3_ref, l1_ref, l2_ref, l3_ref, yb_ref, yc_ref, yd_ref,
                    wg_ref, wb_ref, wo_ref, exp_ref, up2_ref, up3_ref,
                    gn_ref, fg_ref, fu_ref, fd_ref, gf_ref, out_ref):
    hb = h_ref[...]
    unperms = (None, up2_ref[...], up3_ref[...])
    o1, o2, o3 = (_from_residue_view(r, d, p) for r, d, p in zip((o1_ref, o2_ref, o3_ref), DILATIONS, unperms))
    l1, l2, l3 = (_from_residue_view(r, d, p) for r, d, p in zip((l1_ref, l2_ref, l3_ref), DILATIONS, unperms))
    m = jnp.maximum(jnp.maximum(l1, l2), l3)
    e1, e2, e3 = jnp.exp(l1 - m), jnp.exp(l2 - m), jnp.exp(l3 - m)
    inv = 1.0 / (e1 + e2 + e3)
    expand = exp_ref[...]
    ya = (_dot_sel(e1 * inv, expand) * o1 + _dot_sel(e2 * inv, expand) * o2 + _dot_sel(e3 * inv, expand) * o3)
    merged = None
    for n, y in enumerate((ya, yb_ref[...], yc_ref[...], yd_ref[...])):
        gate = _sigmoid(_dot(hb, wg_ref[:, n * D_MODEL:(n + 1) * D_MODEL]))
        term = gate * _dot(y.astype(BF16), wb_ref[n])
        merged = term if merged is None else merged + term
    x = x_ref[...] + _dot(merged.astype(BF16), wo_ref[...])
    h2 = _rms(x, gn_ref[...]).astype(BF16)
    gate = _dot(h2, fg_ref[...])
    up = _dot(h2, fu_ref[...])
    y = x + _dot((gate * _sigmoid(gate) * up).astype(BF16), fd_ref[...])
    out_ref[...] = _rms(y, gf_ref[...]) if final else y


def _merge_ffn(x, h, os, ls, yb, yc, yd, wg, wb, wo, expand, gn, fg, fu, fd, gf, final, tm=256):
    b, s, d = x.shape
    row = lambda wd: pl.BlockSpec((None, tm, wd), lambda bi, i: (bi, i, 0))
    res = lambda wd: [pl.BlockSpec((None, tm // dil, dil * wd), lambda bi, i: (bi, i, 0)) for dil in DILATIONS]
    consts = (wg, wb, wo, expand) + tuple(_dilated_perm(tm, dil).T for dil in DILATIONS[1:]) + (gn, fg, fu, fd, gf)
    return pl.pallas_call(
        functools.partial(_merge_ffn_body, final),
        grid=(b, s // tm),
        in_specs=[row(d), row(d)] + res(WIDTH) + res(LANES)
        + [_seg_view_spec(s, tm), row(WIDTH), row(WIDTH)] + [_const_spec(c.shape) for c in consts],
        out_specs=row(d),
        out_shape=jax.ShapeDtypeStruct((b, s, d), F32),
        compiler_params=_params(("parallel", "parallel")),
        name="merge_ffn_final" if final else "merge_ffn",
    )(x, h, *os, *ls, yb, yc, yd, *consts)


def _rope_tables(s):
    inv = 1.0 / (ROPE_THETA ** (jnp.arange(0, HEAD_DIM, 2, dtype=F32) / HEAD_DIM))
    ang = jnp.arange(s, dtype=F32)[:, None] * inv[None, :]
    cos, sin = jnp.cos(ang), jnp.sin(ang)
    reps = LANES // HEAD_DIM
    cos_t = jnp.tile(jnp.concatenate([cos, cos], axis=-1), (1, reps))
    sin_t = jnp.tile(jnp.concatenate([-sin, sin], axis=-1), (1, reps))
    return cos_t, sin_t


def _block_diag(blocks):
    g, r, c = blocks.shape
    eye = jnp.eye(g, dtype=blocks.dtype)
    return (eye[:, None, :, None] * blocks[:, :, None, :]).reshape(g * r, g * c)


def _s5_params(lam_re, lam_im, log_dt, b_re, b_im, c_re, c_im):
    lr, li = lam_re.astype(F32), lam_im.astype(F32)
    dt = jnp.exp(log_dt.astype(F32))[:, None]
    mag = jnp.exp(lr * dt)
    ab_re, ab_im = mag * jnp.cos(li * dt), mag * jnp.sin(li * dt)
    nr = ab_re - 1.0
    den = lr * lr + li * li
    f_re = (nr * lr + ab_im * li) / den
    f_im = (ab_im * lr - nr * li) / den
    bb_re = f_re[..., None] * b_re - f_im[..., None] * b_im
    bb_im = f_re[..., None] * b_im + f_im[..., None] * b_re
    gh = S5_GROUPS // S5_HALVES
    swap = lambda t: jnp.transpose(t, (0, 2, 1))
    halves = lambda t: [t[h * gh:(h + 1) * gh] for h in range(S5_HALVES)]
    bbd = jnp.stack([jnp.concatenate([_block_diag(swap(re)), _block_diag(swap(im))], axis=-1)
                     for re, im in zip(halves(bb_re), halves(bb_im))]).astype(BF16)
    cbd = jnp.stack([jnp.concatenate([_block_diag(swap(re)), -_block_diag(swap(im))], axis=0)
                     for re, im in zip(halves(c_re), halves(c_im))]).astype(BF16)
    flat = lambda t: [h.reshape(1, -1) for h in halves(t)]
    cat = lambda parts: jnp.concatenate(parts, axis=-1)
    a_plain = cat([cat([re, im]) for re, im in zip(flat(ab_re), flat(ab_im))])
    a_same = cat([cat([re, re]) for re in flat(ab_re)])
    a_cross = cat([cat([-im, im]) for im in flat(ab_im)])
    rep = lambda t: jnp.broadcast_to(t, (S5_SEGS, t.shape[-1]))
    return bbd, rep(a_same), rep(a_cross), a_plain, cbd


def _lora_weights(mu_wag, w1, a1, g1, mu_v, v1):
    d = w1.shape[0]
    pad = lambda t: jnp.pad(t, ((0, 0), (0, LORA_HALF - t.shape[1])))
    v1 = jnp.zeros((d, 32), F32) if v1 is None else v1
    mu_v = jnp.zeros((d,), F32) if mu_v is None else mu_v
    mats = (w1, a1, g1, v1)
    mus = (mu_wag[0], mu_wag[1], mu_wag[2], mu_v)
    keep = pad(jnp.concatenate([m * (1.0 - mu)[:, None] for m, mu in zip(mats, mus)], axis=1))
    shifted = pad(jnp.concatenate([m * mu[:, None] for m, mu in zip(mats, mus)], axis=1))
    return jnp.concatenate([keep, shifted], axis=1).astype(BF16)


def kernel(x, norm_mix, w_in, s5_lam_re, s5_lam_im, s5_log_dt, s5_b_re, s5_b_im, s5_c_re, s5_c_im, s5_d, s5_w_glu, s5_b_glu, rw_mu_rkv, rw_mu_wag, rw_w0, rw_w1, rw_w2, rw_a0, rw_a1, rw_a2, rw_g1, rw_g2, rw_k_k, rw_k_a, rw_r_k, rw_gn_w, rw_gn_b, rw_mu_v, rw_v0, rw_v1, rw_v2, lru_conv_w, lru_conv_b, lru_w_a, lru_b_a, lru_w_x, lru_b_x, lru_lam, w_branch, w_out, norm_ffn, w_ffn_gate, w_ffn_up, w_ffn_down, norm_final):
    b, s, d = x.shape
    depth = w_in.shape[0]
    w = WIDTH
    cos_t, sin_t = _rope_tables(s)
    head_id = jnp.arange(w) // HEAD_DIM
    hsum = (head_id[:, None] == head_id[None, :]).astype(BF16)
    expand = (jnp.arange(LANES)[:, None] == head_id[None, :]).astype(BF16)
    bf = lambda t: t.astype(BF16)
    row = lambda t: t.reshape(1, -1).astype(F32)

    v_first = None
    for l in range(depth):
        wl = w_in[l]
        cols = np.cumsum((0, w, w, w, w, w, w, w, w))
        seg = lambda i, j: bf(wl[:, cols[i]:cols[j]])
        has_vres = l > 0
        wlora = _lora_weights(rw_mu_wag[l], rw_w1[l], rw_a1[l], rw_g1[l],
                              rw_mu_v[l - 1] if has_vres else None, rw_v1[l - 1] if has_vres else None)
        lru_consts = (lru_conv_w[l].astype(F32), row(lru_conv_b[l]), bf(_block_diag(lru_w_a[l])), row(lru_b_a[l]),
                      bf(_block_diag(lru_w_x[l])), row(lru_b_x[l]), row(lru_lam[l]))
        h, u, rkv, y_d, lora, *qkv_views = _inproj(
            x, row(norm_mix[l]), cos_t, sin_t, seg(0, 2), seg(2, 3), seg(3, 4), seg(4, 7), seg(7, 8), wlora, lru_consts)

        att = [_attention(view, dil) for view, dil in zip(qkv_views, DILATIONS)]

        s5p = _s5_params(s5_lam_re[l], s5_lam_im[l], s5_log_dt[l], s5_b_re[l], s5_b_im[l], s5_c_re[l], s5_c_im[l])
        y_b = _s5(u, *s5p, row(s5_d[l]), bf(s5_w_glu[l]), row(s5_b_glu[l]))

        zero = jnp.zeros((w,), F32)
        vec = jnp.stack([rw_mu_rkv[l, 0], rw_mu_rkv[l, 1], rw_mu_rkv[l, 2], rw_w0[l], rw_a0[l],
                         rw_v0[l - 1] if has_vres else zero, rw_k_k[l], rw_k_a[l], rw_r_k[l].reshape(-1),
                         rw_gn_w[l], rw_gn_b[l]] + [zero] * 5).astype(F32)
        v2 = bf(jnp.pad(rw_v2[l - 1], ((0, 32), (0, 0)))) if has_vres else None
        y_c, v_c = _rwkv(rkv, lora, v_first, vec, bf(rw_w2[l]), bf(rw_a2[l]), bf(rw_g2[l]), v2, hsum)
        if l == 0:
            v_first = v_c

        x = _merge_ffn(x, h, [o for o, _ in att], [ls for _, ls in att], y_b, y_c, y_d,
                       bf(wl[:, cols[8]:]), bf(w_branch[l]), bf(w_out[l]), expand,
                       row(norm_ffn[l]), bf(w_ffn_gate[l]), bf(w_ffn_up[l]), bf(w_ffn_down[l]), row(norm_final),
                       final=(l == depth - 1))
    return x
```

```python
import functools
import math

import jax
import jax.numpy as jnp
import numpy as np
from jax import lax
from jax.experimental import pallas as pl
from jax.experimental.pallas import tpu as pltpu

F32 = jnp.float32
BF16 = jnp.bfloat16

D_MODEL = 1024
N_HEADS = 8
HEAD_DIM = 64
WIDTH = N_HEADS * HEAD_DIM
DILATIONS = (1, 4, 16)
ATT_BLOCK = 128
ROPE_THETA = 10000.0
S5_GROUPS = 32
S5_GROUP = 16
S5_STATE = 64
S5_NSTATE = S5_GROUPS * S5_STATE
RWKV_GN_EPS = 64e-5
RWKV_CHUNK = 64
LORA_HALF = 384
LRU_BLOCKS = 8
LRU_C = 8.0
D_FF = 2816
RMS_EPS = 1e-6
NEG_BIG = -1e30

LANES = 128
SUBLANES = 8
S5_SEGS = SUBLANES
S5_HALVES = 2
S5_HALF_STATES = S5_NSTATE // S5_HALVES
VMEM_LIMIT = 56 * 1024 * 1024

NT_DIMS = (((1,), (1,)), ((), ()))


def _dot(a, b):
    return jnp.dot(a, b, preferred_element_type=F32)


def _dot_nt(a, b):
    return lax.dot_general(a, b, NT_DIMS, preferred_element_type=F32)


def _split2(x):
    hi = x.astype(BF16)
    lo = (x - hi.astype(F32)).astype(BF16)
    return hi, lo


def _split3(x):
    hi = x.astype(BF16)
    r1 = x - hi.astype(F32)
    mid = r1.astype(BF16)
    lo = (r1 - mid.astype(F32)).astype(BF16)
    return hi, mid, lo


def _dot_sel(x, sel):
    hi, lo = _split2(x)
    return _dot(hi, sel) + _dot(lo, sel)


def _head_sums(xs, hsum):
    m = xs[0].shape[0]
    parts = []
    for x in xs:
        parts += list(_split2(x))
    stacked = jnp.concatenate(parts, axis=0)
    half = hsum.shape[0] // 2
    block = hsum[:half, :half]
    out = jnp.concatenate([_dot(stacked[:, :half], block), _dot(stacked[:, half:], block)], axis=-1)
    return [out[2 * j * m:(2 * j + 1) * m] + out[(2 * j + 1) * m:(2 * j + 2) * m] for j in range(len(xs))]


def _sigmoid(x):
    return 0.5 * jnp.tanh(0.5 * x) + 0.5


def _softplus(z):
    return jnp.maximum(z, 0.0) + jnp.log1p(jnp.exp(-jnp.abs(z)))


def _rms(x, g):
    ms = jnp.mean(x * x, axis=-1, keepdims=True)
    return x * lax.rsqrt(ms + RMS_EPS) * g


def _const_spec(shape):
    n = len(shape)
    return pl.BlockSpec(shape, lambda *_: (0,) * n, pipeline_mode=pl.Buffered(1))


def _params(sem):
    return pltpu.CompilerParams(dimension_semantics=sem, vmem_limit_bytes=VMEM_LIMIT)


def _seg_view_spec(s, tm):
    tiles_per_seg = s // S5_SEGS // tm
    return pl.BlockSpec((None, tm, WIDTH), lambda bi, i: (bi, i % tiles_per_seg, i // tiles_per_seg))


def _dilated_perm(tm, dil):
    dst = jnp.arange(tm)
    src = (dst % (tm // dil)) * dil + dst // (tm // dil)
    return (src[:, None] == jnp.arange(tm)[None, :]).astype(BF16)


def _lru_apply(x, cw_ref, cb_ref, wa_ref, ba_ref, wx_ref, bx_ref, lam_ref, o_ref, a_ref, carry_ref, h_ref):
    tb = x.shape[0]
    full = jnp.concatenate([carry_ref[...], x], axis=0)
    carry_ref[...] = x[tb - SUBLANES:]
    cw = cw_ref[...]
    taps = cw.shape[0]
    xc = cb_ref[...] + cw[taps - 1:taps, :] * x
    for j in range(1, taps):
        xc = xc + cw[taps - 1 - j:taps - j, :] * pltpu.roll(full, j, 0)[SUBLANES:]
    xcb = xc.astype(BF16)
    r = _sigmoid(_dot(xcb, wa_ref[...]) + ba_ref[...])
    i = _sigmoid(_dot(xcb, wx_ref[...]) + bx_ref[...])
    log_a = -LRU_C * r * _softplus(-lam_ref[...])
    a = jnp.exp(log_a)
    bb = jnp.sqrt(-jnp.tanh(log_a) * (a * a + 1.0)) * (i * xc)
    in_tile = lax.broadcasted_iota(jnp.int32, (tb, 1), 0) % SUBLANES
    for k in (1, 2, 4):
        keep = in_tile >= k
        bb = jnp.where(keep, a * pltpu.roll(bb, k, 0) + bb, bb)
        a = jnp.where(keep, a * pltpu.roll(a, k, 0), a)
    a_ref[...] = a
    o_ref[...] = bb
    h = h_ref[0:1, :]
    for n in range(tb // SUBLANES):
        rows = slice(n * SUBLANES, (n + 1) * SUBLANES)
        res = a_ref[rows, :] * h + o_ref[rows, :]
        o_ref[rows, :] = res
        h = res[SUBLANES - 1:SUBLANES, :]
    h_ref[0:1, :] = h


def _inproj_body(x_ref, g_ref, cos_ref, sin_ref, wqk_ref, wv_ref, wu_ref, wrkv_ref, wlru_ref, wlora_ref, *rest):
    n_dil = len(DILATIONS) - 1
    lru_consts, rest = rest[:7], rest[7:]
    perm_refs, (h_ref, u_ref, rkv_ref, lru_ref, lora_ref, qkv_ref), rest = rest[:n_dil], rest[n_dil:n_dil + 6], rest[n_dil + 6:]
    dil_refs, lru_scratch = rest[:n_dil], rest[n_dil:]
    tm = x_ref.shape[0]

    @pl.when(pl.program_id(1) == 0)
    def _():
        for ref in lru_scratch[1:]:
            ref[...] = jnp.zeros_like(ref)

    hb = _rms(x_ref[...], g_ref[...]).astype(BF16)
    h_ref[...] = hb
    _lru_apply(_dot(hb, wlru_ref[...]), *lru_consts, lru_ref, *lru_scratch)
    qk = _dot(hb, wqk_ref[...])
    reps = 2 * WIDTH // LANES
    cos = jnp.concatenate([cos_ref[...]] * reps, axis=-1)
    sin = jnp.concatenate([sin_ref[...]] * reps, axis=-1)
    lane = lax.broadcasted_iota(jnp.int32, qk.shape, 1)
    first_half = (lane % HEAD_DIM) < (HEAD_DIM // 2)
    partner = jnp.where(first_half,
                        pltpu.roll(qk, 2 * WIDTH - HEAD_DIM // 2, 1),
                        pltpu.roll(qk, HEAD_DIM // 2, 1))
    rot = qk * cos + partner * sin
    qkv = jnp.concatenate([(rot[:, :WIDTH] * (HEAD_DIM ** -0.5)).astype(BF16), rot[:, WIDTH:].astype(BF16),
                           _dot(hb, wv_ref[...]).astype(BF16)], axis=-1)
    qkv_ref[...] = qkv
    for dil, perm_ref, out_ref in zip(DILATIONS[1:], perm_refs, dil_refs):
        grouped = _dot(perm_ref[...], qkv).astype(BF16)
        rows = tm // dil
        for r in range(dil):
            out_ref[:, r * 3 * WIDTH:(r + 1) * 3 * WIDTH] = grouped[r * rows:(r + 1) * rows]
    u_ref[...] = _dot(hb, wu_ref[...])
    rkv_ref[...] = _dot(hb, wrkv_ref[...])
    lora_ref[...] = _dot(hb, wlora_ref[...])


def _inproj(x, g, cos, sin, wqk, wv, wu, wrkv, wlru, wlora, lru_consts, tm=256):
    b, s, d = x.shape
    grid = (b, s // tm)
    row = lambda w: pl.BlockSpec((None, tm, w), lambda bi, i: (bi, i, 0))
    tab = pl.BlockSpec((tm, LANES), lambda bi, i: (i, 0))
    widths = (d, WIDTH, 3 * WIDTH, WIDTH, 2 * LORA_HALF, 3 * WIDTH)
    dtypes = (BF16, F32, F32, F32, F32, BF16)
    out_specs = [row(w) for w in widths]
    out_shape = [jax.ShapeDtypeStruct((b, s, w), dt) for w, dt in zip(widths, dtypes)]
    out_specs[1] = _seg_view_spec(s, tm)
    out_shape[1] = jax.ShapeDtypeStruct((b, s // S5_SEGS, S5_SEGS * WIDTH), F32)
    for dil in DILATIONS[1:]:
        out_specs.append(pl.BlockSpec((None, tm // dil, dil * 3 * WIDTH), lambda bi, i: (bi, i, 0)))
        out_shape.append(jax.ShapeDtypeStruct((b, s // dil, dil * 3 * WIDTH), BF16))
    consts = (wqk, wv, wu, wrkv, wlru, wlora) + tuple(lru_consts) + tuple(_dilated_perm(tm, dil) for dil in DILATIONS[1:])
    return pl.pallas_call(
        _inproj_body,
        grid=grid,
        in_specs=[row(d), _const_spec((1, d)), tab, tab] + [_const_spec(c.shape) for c in consts],
        out_specs=out_specs,
        out_shape=out_shape,
        scratch_shapes=[pltpu.VMEM((tm, WIDTH), F32), pltpu.VMEM((SUBLANES, WIDTH), F32), pltpu.VMEM((SUBLANES, WIDTH), F32)],
        compiler_params=_params(("parallel", "arbitrary")),
        name="inproj",
    )(x, g, cos, sin, *consts)


def _attn_body(q_ref, kc_ref, kp_ref, vc_ref, o_ref, l_ref, vt_ref):
    i = pl.program_id(2)
    blk = ATT_BLOCK
    n_sub = q_ref.shape[0] // blk

    @pl.when(i == 0)
    def _():
        vt_ref[...] = jnp.zeros_like(vt_ref)

    rows = [slice(j * blk, (j + 1) * blk) for j in range(n_sub)]
    vts = [vt_ref[...]] + [vc_ref[r, :].astype(F32).T.astype(BF16) for r in rows]
    vt_ref[...] = vts[-1]
    key = lax.broadcasted_iota(jnp.int32, (2 * blk, blk), 0)
    qry = lax.broadcasted_iota(jnp.int32, (2 * blk, blk), 1)
    in_cur = jnp.logical_and(key >= blk, key - blk <= qry)
    in_prev = jnp.logical_and(key < blk, key >= qry)
    masks = [jnp.logical_or(in_cur, jnp.logical_and(in_prev, i > 0))] + [jnp.logical_or(in_cur, in_prev)] * (n_sub - 1)
    heads = [slice(h * HEAD_DIM, (h + 1) * HEAD_DIM) for h in range(N_HEADS)]
    pairs = [(j, sl) for j in range(n_sub) for sl in heads]
    prev_keys = lambda j, sl: kp_ref[:, sl] if j == 0 else kc_ref[rows[j - 1], sl]
    keys = [jnp.concatenate([prev_keys(j, sl), kc_ref[rows[j], sl]], axis=0) for j, sl in pairs]
    st = [jnp.where(masks[j], _dot_nt(k, q_ref[rows[j], sl]), NEG_BIG) for k, (j, sl) in zip(keys, pairs)]
    ms = [jnp.max(s, axis=0, keepdims=True) for s in st]
    ps = [jnp.exp(s - m) for s, m in zip(st, ms)]
    dens = [jnp.sum(p, axis=0, keepdims=True) for p in ps]
    outs = []
    for n, (j, sl) in enumerate(pairs):
        vt = jnp.concatenate([vts[j][sl, :], vts[j + 1][sl, :]], axis=1)
        outs.append(_dot(vt, ps[n].astype(BF16)) * (1.0 / dens[n]))
    pad = [jnp.zeros((LANES - N_HEADS, blk), F32)]
    for j in range(n_sub):
        mine = slice(j * N_HEADS, (j + 1) * N_HEADS)
        o_ref[rows[j], :] = jnp.concatenate(outs[mine], axis=0).T.astype(BF16)
        lse = jnp.concatenate([m + jnp.log(d) for m, d in zip(ms[mine], dens[mine])] + pad, axis=0)
        l_ref[rows[j], :] = lse.T


def _attention(qkv_view, dil):
    b, sub, _ = qkv_view.shape
    w = WIDTH
    n_blocks = sub // ATT_BLOCK
    n_sub = max(k for k in (4, 2, 1) if n_blocks % k == 0)
    step = n_sub * ATT_BLOCK
    cur = lambda j: pl.BlockSpec((None, step, w), lambda bi, r, i: (bi, i, 3 * r + j))
    prev_k = pl.BlockSpec((None, ATT_BLOCK, w), lambda bi, r, i: (bi, jnp.maximum(i * n_sub - 1, 0), 3 * r + 1))
    return pl.pallas_call(
        _attn_body,
        grid=(b, dil, n_blocks // n_sub),
        in_specs=[cur(0), cur(1), prev_k, cur(2)],
        out_specs=[pl.BlockSpec((None, step, w), lambda bi, r, i: (bi, i, r)),
                   pl.BlockSpec((None, step, LANES), lambda bi, r, i: (bi, i, r))],
        out_shape=[jax.ShapeDtypeStruct((b, sub, dil * w), BF16),
                   jax.ShapeDtypeStruct((b, sub, dil * LANES), F32)],
        scratch_shapes=[pltpu.VMEM((w, ATT_BLOCK), BF16)],
        compiler_params=_params(("parallel", "parallel", "arbitrary")),
        name=f"attn_d{dil}",
    )(*([qkv_view] * 4))


def _gelu_tanh(x):
    return 0.5 * x * (1.0 + jnp.tanh(math.sqrt(2.0 / math.pi) * (x + 0.044715 * (x * x * x))))


def _s5_slab(u_ref, perm_ref):
    u_stack = jnp.concatenate([u_ref[:, sg * WIDTH:(sg + 1) * WIDTH] for sg in range(S5_SEGS)], axis=0)
    u_slab = _dot(perm_ref[...], u_stack.astype(BF16)).astype(BF16)
    return u_stack, u_slab


def _s5_scan(u_slab, bbd_ref, asame_ref, across_ref, bu_ref, st_ref, store):
    hs = S5_HALF_STATES
    cin = WIDTH // S5_HALVES
    cols = [slice(half * 2 * hs, (half + 1) * 2 * hs) for half in range(S5_HALVES)]
    for half in range(S5_HALVES):
        bu_ref[half] = _dot(u_slab[:, half * cin:(half + 1) * cin], bbd_ref[half])
    xs = [st_ref[:, c] for c in cols]
    for t in range(bu_ref.shape[1] // S5_SEGS):
        rows = slice(t * S5_SEGS, (t + 1) * S5_SEGS)
        for half, c in enumerate(cols):
            x = xs[half]
            swapped = jnp.concatenate([x[:, hs:], x[:, :hs]], axis=-1)
            x = asame_ref[:, c] * x + across_ref[:, c] * swapped + bu_ref[half, rows, :]
            if store:
                bu_ref[half, rows, :] = x
            xs[half] = x
    for half, c in enumerate(cols):
        st_ref[:, c] = xs[half]


def _s5_state_body(u_ref, perm_ref, bbd_ref, asame_ref, across_ref, end_ref, bu_ref, st_ref):
    @pl.when(pl.program_id(1) == 0)
    def _():
        st_ref[...] = jnp.zeros_like(st_ref)

    _, u_slab = _s5_slab(u_ref, perm_ref)
    _s5_scan(u_slab, bbd_ref, asame_ref, across_ref, bu_ref, st_ref, store=False)

    @pl.when(pl.program_id(1) == pl.num_programs(1) - 1)
    def _():
        end_ref[...] = st_ref[...]


def _cmul(z, w):
    hs = z.shape[-1] // 2
    zr, zi, wr, wi = z[:, :hs], z[:, hs:], w[:, :hs], w[:, hs:]
    return jnp.concatenate([zr * wr - zi * wi, zr * wi + zi * wr], axis=-1)


def _s5_out_body(seg_len, u_ref, end_ref, perm_ref, permt_ref, bbd_ref, asame_ref, across_ref, aplain_ref,
                 cbd_ref, d_ref, wglu_ref, bglu_ref, o_ref, bu_ref, st_ref):
    hs = S5_HALF_STATES
    tt = u_ref.shape[0]

    @pl.when(pl.program_id(1) == 0)
    def _():
        for half in range(S5_HALVES):
            cols = slice(half * 2 * hs, (half + 1) * 2 * hs)
            base = aplain_ref[:, cols]
            power = None
            e = seg_len
            while e:
                if e & 1:
                    power = base if power is None else _cmul(power, base)
                e >>= 1
                if e:
                    base = _cmul(base, base)
            x = jnp.zeros((1, 2 * hs), F32)
            for sg in range(S5_SEGS):
                st_ref[sg:sg + 1, cols] = x
                x = _cmul(x, power) + end_ref[sg:sg + 1, cols]

    u_stack, u_slab = _s5_slab(u_ref, perm_ref)
    _s5_scan(u_slab, bbd_ref, asame_ref, across_ref, bu_ref, st_ref, store=True)
    ys = [_dot(bu_ref[half].astype(BF16), cbd_ref[half]) for half in range(S5_HALVES)]
    hi, lo = _split2(jnp.concatenate(ys, axis=-1))
    y = _dot(permt_ref[...], hi) + _dot(permt_ref[...], lo) + d_ref[...] * u_stack
    z = _dot(_gelu_tanh(y).astype(BF16), wglu_ref[...]) + bglu_ref[...]
    out = z[:, :WIDTH] * _sigmoid(z[:, WIDTH:])
    for sg in range(S5_SEGS):
        o_ref[:, sg * WIDTH:(sg + 1) * WIDTH] = out[sg * tt:(sg + 1) * tt]


def _s5(u_view, bbd, a_same, a_cross, a_plain, cbd, d_skip, wglu, bglu, tt=64):
    b, seg_len, wide = u_view.shape
    m = tt * S5_SEGS
    dst = jnp.arange(m)
    src = (dst % S5_SEGS) * tt + dst // S5_SEGS
    perm = (src[:, None] == jnp.arange(m)[None, :]).astype(BF16)
    row = pl.BlockSpec((None, tt, wide), lambda bi, i: (bi, i, 0))
    ends = pl.BlockSpec((None, S5_SEGS, 2 * S5_NSTATE), lambda bi, i: (bi, 0, 0))
    scratch = [pltpu.VMEM((S5_HALVES, m, 2 * S5_HALF_STATES), F32), pltpu.VMEM((S5_SEGS, 2 * S5_NSTATE), F32)]
    consts1 = (perm, bbd, a_same, a_cross)
    seg_end = pl.pallas_call(
        _s5_state_body,
        grid=(b, seg_len // tt),
        in_specs=[row] + [_const_spec(c.shape) for c in consts1],
        out_specs=ends,
        out_shape=jax.ShapeDtypeStruct((b, S5_SEGS, 2 * S5_NSTATE), F32),
        scratch_shapes=scratch,
        compiler_params=_params(("parallel", "arbitrary")),
        name="s5_state",
    )(u_view, *consts1)
    consts2 = (perm, perm.T, bbd, a_same, a_cross, a_plain, cbd, d_skip, wglu, bglu)
    return pl.pallas_call(
        functools.partial(_s5_out_body, seg_len),
        grid=(b, seg_len // tt),
        in_specs=[row, ends] + [_const_spec(c.shape) for c in consts2],
        out_specs=row,
        out_shape=jax.ShapeDtypeStruct((b, seg_len, wide), F32),
        scratch_shapes=scratch,
        compiler_params=_params(("parallel", "arbitrary")),
        name="s5_out",
    )(u_view, seg_end, *consts2)


V_MU_R, V_MU_K, V_MU_V, V_W0, V_A0, V_V0, V_KK, V_KA, V_RK, V_GNW, V_GNB = range(11)


def _rwkv_body(has_vres, *refs):
    if has_vres:
        (rkv_ref, lora_ref, vf_ref, vec_ref, w2_ref, a2_ref, g2_ref, v2_ref, hsum_ref,
         y_ref, st_ref, carry_ref, ysc_ref) = refs
    else:
        (rkv_ref, lora_ref, vec_ref, w2_ref, a2_ref, g2_ref, hsum_ref,
         y_ref, vout_ref, st_ref, carry_ref, ysc_ref) = refs
    L = RWKV_CHUNK
    nb, nj = rkv_ref.shape[0], rkv_ref.shape[1] // L
    m = nb * nj * L
    w = WIDTH

    @pl.when(pl.program_id(0) == 0)
    def _():
        st_ref[...] = jnp.zeros_like(st_ref)
        carry_ref[...] = jnp.zeros_like(carry_ref)

    vec = vec_ref[...]
    row = lambda j: vec[j:j + 1, :]
    hsum = hsum_ref[...]

    rkv = rkv_ref[...].reshape(m, 3 * w)
    lora = lora_ref[...].reshape(m, 2 * LORA_HALF)
    cat = jnp.concatenate([rkv, lora[:, LORA_HALF:]], axis=-1)
    row_id = lax.broadcasted_iota(jnp.int32, (m, 1), 0)
    prev = pltpu.roll(cat, 1, 0)
    for b in range(nb):
        prev = jnp.where(row_id == b * nj * L, carry_ref[b], prev)
        carry_ref[b] = cat[(b + 1) * nj * L - 1:(b + 1) * nj * L, :]
    lerp = lambda j, mu: rkv[:, j * w:(j + 1) * w] + (prev[:, j * w:(j + 1) * w] - rkv[:, j * w:(j + 1) * w]) * mu
    r = lerp(0, row(V_MU_R))
    k = lerp(1, row(V_MU_K))
    v = lerp(2, row(V_MU_V))
    pre = lora[:, :LORA_HALF] + prev[:, 3 * w:]

    wx = row(V_W0) + _dot(jnp.tanh(pre[:, 0:64]).astype(BF16), w2_ref[...])
    logw = -jnp.exp(-_softplus(-wx) - 0.5)
    a = _sigmoid(row(V_A0) + _dot(pre[:, 64:128].astype(BF16), a2_ref[...]))
    g = _dot(_sigmoid(pre[:, 128:256]).astype(BF16), g2_ref[...])
    if has_vres:
        vf = vf_ref[...].reshape(m, w)
        v = v + (vf - v) * _sigmoid(row(V_V0) + _dot(pre[:, 256:320].astype(BF16), v2_ref[...]))
    else:
        vout_ref[...] = v.reshape(nb, nj * L, w)

    kk = k * row(V_KK)
    k = k * (1.0 + (a - 1.0) * row(V_KA))
    kk_sq, rk_sum = _head_sums([kk * kk, r * k * row(V_RK)], hsum)
    kk = kk * lax.rsqrt(jnp.maximum(kk_sq, 1e-24))
    bonus = rk_sum * v

    pair = 2 * L
    mi = lax.broadcasted_iota(jnp.int32, (pair, pair), 0)
    mj = lax.broadcasted_iota(jnp.int32, (pair, pair), 1)
    same_chunk = (mi // L) == (mj // L)
    tri = jnp.where(jnp.logical_and(same_chunk, mj <= mi), 1.0, 0.0).astype(BF16)
    parts = _split3(logw)
    cl = jnp.concatenate([sum(_dot(tri, part[lo:lo + pair]) for part in parts) for lo in range(0, m, pair)], axis=0)
    e_neg = jnp.exp(-cl)
    a_t = ((-kk) * jnp.exp(cl - logw)).astype(BF16)
    b_t = ((kk * a) * e_neg).astype(BF16)
    k_t = (k * e_neg).astype(BF16)
    r_t = (r * jnp.exp(cl)).astype(BF16)
    v_b = v.astype(BF16)

    pw_lanes = 2 * HEAD_DIM
    lane = lax.broadcasted_iota(jnp.int32, (L, pw_lanes), 1)
    step = lax.broadcasted_iota(jnp.int32, (L, pw_lanes), 0)
    head0 = lane < HEAD_DIM
    upper_strict = step < lane % HEAD_DIM
    lower_incl = lane % HEAD_DIM <= step
    eye = (step == lane % HEAD_DIM).astype(F32)
    n_double = int(math.log2(L)) - 1

    def per_head_rows(x):
        zero = jnp.zeros_like(x)
        return jnp.concatenate([jnp.where(head0, x, zero), jnp.where(head0, zero, x)], axis=0)

    chains = []
    for b in range(nb):
        for j in range(nj):
            lo = (b * nj + j) * L
            rows = slice(lo, lo + L)
            v_tt = v[rows].T.astype(BF16)
            p_end = jnp.exp(cl[lo + L - 1:lo + L, :])
            for p in range(N_HEADS // 2):
                sl = slice(p * pw_lanes, (p + 1) * pw_lanes)
                vtt = jnp.concatenate([v_tt[p * pw_lanes:p * pw_lanes + HEAD_DIM, :],
                                       v_tt[p * pw_lanes + HEAD_DIM:(p + 1) * pw_lanes, :]], axis=1)
                chains.append(dict(b=b, j=j, p=p, rows=rows, sl=sl, a_bd=per_head_rows(a_t[rows, sl]),
                                   b_bd=per_head_rows(b_t[rows, sl]), k_bd=per_head_rows(k_t[rows, sl]),
                                   bk=jnp.concatenate([b_t[rows, sl], k_t[rows, sl]], axis=0),
                                   r=r_t[rows, sl], v_bd=per_head_rows(v_b[rows, sl]), vtt=vtt, p_end=p_end[:, sl]))
    for c in chains:
        both = _dot_nt(c["bk"], c["a_bd"])
        c["tab"] = jnp.where(upper_strict, both[:L], 0.0)
        c["tak_bd"] = per_head_rows(jnp.where(upper_strict, both[L:], 0.0).astype(BF16))
        rbk = _dot_nt(c["r"], jnp.concatenate([c["b_bd"], c["k_bd"]], axis=0))
        c["trb"] = jnp.where(lower_incl, rbk[:, :pw_lanes], 0.0).astype(BF16)
        c["trk"] = jnp.where(lower_incl, rbk[:, pw_lanes:], 0.0).astype(BF16)
    for c in chains:
        c["inv"] = eye + c["tab"]
        tb16 = c["tab"].astype(BF16)
        c["pw"] = _dot(tb16, per_head_rows(tb16)).astype(BF16)
    for it in range(n_double):
        for c in chains:
            pw_bd = per_head_rows(c["pw"])
            if it < n_double - 1:
                prod = _dot(jnp.concatenate([c["inv"].astype(BF16), c["pw"]], axis=0), pw_bd)
                c["inv"] = c["inv"] + prod[:L]
                c["pw"] = prod[L:].astype(BF16)
            else:
                c["inv"] = c["inv"] + _dot(c["inv"].astype(BF16), pw_bd)
    for c in chains:
        c["inv_bd"] = per_head_rows(c["inv"].astype(BF16))
    state = {(b, p): st_ref[b, p] for b in range(nb) for p in range(N_HEADS // 2)}
    for j in range(nj):
        group = [c for c in chains if c["j"] == j]
        for c in group:
            c["st"] = state[c["b"], c["p"]]
            c["stb"] = c["st"].astype(BF16)
            c["rhs"] = _dot_nt(c["stb"], c["a_bd"]) + _dot(c["vtt"], c["tak_bd"])
        for c in group:
            c["ut"] = _dot(c["rhs"].astype(BF16), c["inv_bd"]).astype(BF16)
        for c in group:
            state[c["b"], c["p"]] = (c["st"] + _dot(c["ut"], c["b_bd"]) + _dot(c["vtt"], c["k_bd"])) * c["p_end"]
        for c in group:
            lhs = jnp.concatenate([c["r"], c["trb"]], axis=1)
            rhs = jnp.concatenate([per_head_rows(c["stb"]), per_head_rows(c["ut"])], axis=1)
            ysc_ref[c["rows"], c["sl"]] = _dot_nt(lhs, rhs) + _dot(c["trk"], c["v_bd"])
    for (b, p), value in state.items():
        st_ref[b, p] = value

    y = ysc_ref[...]
    mean = _head_sums([y], hsum)[0] * (1.0 / HEAD_DIM)
    dev = y - mean
    var = _head_sums([dev * dev], hsum)[0] * (1.0 / HEAD_DIM)
    y = dev * lax.rsqrt(var + RWKV_GN_EPS) * row(V_GNW) + row(V_GNB)
    y_ref[...] = ((y + bonus) * g).reshape(nb, nj * L, w)


def _rwkv(rkv, lora, v_first, vec, w2, a2, g2, v2, hsum):
    b, s, _ = rkv.shape
    n_chunks = s // RWKV_CHUNK
    L = RWKV_CHUNK * max(k for k in (4, 2, 1) if n_chunks % k == 0)
    has_vres = v_first is not None
    row = lambda wd: pl.BlockSpec((b, L, wd), lambda i: (0, i, 0))
    ins = [rkv, lora] + ([v_first] if has_vres else []) + [vec, w2, a2, g2] + ([v2] if has_vres else []) + [hsum]
    in_specs = [row(3 * WIDTH), row(2 * LORA_HALF)] + ([row(WIDTH)] if has_vres else [])
    in_specs += [_const_spec(c.shape) for c in ins[len(in_specs):]]
    y_shape = jax.ShapeDtypeStruct((b, s, WIDTH), F32)
    out = pl.pallas_call(
        functools.partial(_rwkv_body, has_vres),
        grid=(s // L,),
        in_specs=in_specs,
        out_specs=row(WIDTH) if has_vres else [row(WIDTH), row(WIDTH)],
        out_shape=y_shape if has_vres else [y_shape, y_shape],
        scratch_shapes=[pltpu.VMEM((b, N_HEADS // 2, HEAD_DIM, 2 * HEAD_DIM), F32),
                        pltpu.VMEM((b, 1, 3 * WIDTH + LORA_HALF), F32),
                        pltpu.VMEM((b * L, WIDTH), F32)],
        compiler_params=_params(("arbitrary",)),
        name="rwkv_l1" if has_vres else "rwkv_l0",
    )(*ins)
    return (out, v_first) if has_vres else (out[0], out[1])


def _from_residue_view(ref, dil, unperm):
    if dil == 1:
        return ref[...].astype(F32)
    width = ref.shape[1] // dil
    stacked = jnp.concatenate([ref[:, r * width:(r + 1) * width] for r in range(dil)], axis=0)
    if stacked.dtype == BF16:
        return _dot(unperm, stacked)
    hi, lo = _split2(stacked)
    return _dot(unperm, hi) + _dot(unperm, lo)


def _merge_ffn_body(final, x_ref, h_ref, o1_ref, o2_ref, o3_ref, l1_ref, l2_ref, l3_ref, yb_ref, yc_ref, yd_ref,
                    wg_ref, wb_ref, wo_ref, exp_ref, up2_ref, up3_ref,
                    gn_ref, fg_ref, fu_ref, fd_ref, gf_ref, out_ref):
    hb = h_ref[...]
    unperms = (None, up2_ref[...], up3_ref[...])
    o1, o2, o3 = (_from_residue_view(r, d, p) for r, d, p in zip((o1_ref, o2_ref, o3_ref), DILATIONS, unperms))
    l1, l2, l3 = (_from_residue_view(r, d, p) for r, d, p in zip((l1_ref, l2_ref, l3_ref), DILATIONS, unperms))
    m = jnp.maximum(jnp.maximum(l1, l2), l3)
    e1, e2, e3 = jnp.exp(l1 - m), jnp.exp(l2 - m), jnp.exp(l3 - m)
    inv = 1.0 / (e1 + e2 + e3)
    expand = exp_ref[...]
    ya = (_dot_sel(e1 * inv, expand) * o1 + _dot_sel(e2 * inv, expand) * o2 + _dot_sel(e3 * inv, expand) * o3)
    merged = None
    for n, y in enumerate((ya, yb_ref[...], yc_ref[...], yd_ref[...])):
        gate = _sigmoid(_dot(hb, wg_ref[:, n * D_MODEL:(n + 1) * D_MODEL]))
        term = gate * _dot(y.astype(BF16), wb_ref[n])
        merged = term if merged is None else merged + term
    x = x_ref[...] + _dot(merged.astype(BF16), wo_ref[...])
    h2 = _rms(x, gn_ref[...]).astype(BF16)
    gate = _dot(h2, fg_ref[...])
    up = _dot(h2, fu_ref[...])
    y = x + _dot((gate * _sigmoid(gate) * up).astype(BF16), fd_ref[...])
    out_ref[...] = _rms(y, gf_ref[...]) if final else y


def _merge_ffn(x, h, os, ls, yb, yc, yd, wg, wb, wo, expand, gn, fg, fu, fd, gf, final, tm=256):
    b, s, d = x.shape
    row = lambda wd: pl.BlockSpec((None, tm, wd), lambda bi, i: (bi, i, 0))
    res = lambda wd: [pl.BlockSpec((None, tm // dil, dil * wd), lambda bi, i: (bi, i, 0)) for dil in DILATIONS]
    consts = (wg, wb, wo, expand) + tuple(_dilated_perm(tm, dil).T for dil in DILATIONS[1:]) + (gn, fg, fu, fd, gf)
    return pl.pallas_call(
        functools.partial(_merge_ffn_body, final),
        grid=(b, s // tm),
        in_specs=[row(d), row(d)] + res(WIDTH) + res(LANES)
        + [_seg_view_spec(s, tm), row(WIDTH), row(WIDTH)] + [_const_spec(c.shape) for c in consts],
        out_specs=row(d),
        out_shape=jax.ShapeDtypeStruct((b, s, d), F32),
        compiler_params=_params(("parallel", "parallel")),
        name="merge_ffn_final" if final else "merge_ffn",
    )(x, h, *os, *ls, yb, yc, yd, *consts)


def _rope_tables(s):
    inv = 1.0 / (ROPE_THETA ** (jnp.arange(0, HEAD_DIM, 2, dtype=F32) / HEAD_DIM))
    ang = jnp.arange(s, dtype=F32)[:, None] * inv[None, :]
    cos, sin = jnp.cos(ang), jnp.sin(ang)
    reps = LANES // HEAD_DIM
    cos_t = jnp.tile(jnp.concatenate([cos, cos], axis=-1), (1, reps))
    sin_t = jnp.tile(jnp.concatenate([-sin, sin], axis=-1), (1, reps))
    return cos_t, sin_t


def _block_diag(blocks):
    g, r, c = blocks.shape
    eye = jnp.eye(g, dtype=blocks.dtype)
    return (eye[:, None, :, None] * blocks[:, :, None, :]).reshape(g * r, g * c)


def _s5_params(lam_re, lam_im, log_dt, b_re, b_im, c_re, c_im):
    lr, li = lam_re.astype(F32), lam_im.astype(F32)
    dt = jnp.exp(log_dt.astype(F32))[:, None]
    mag = jnp.exp(lr * dt)
    ab_re, ab_im = mag * jnp.cos(li * dt), mag * jnp.sin(li * dt)
    nr = ab_re - 1.0
    den = lr * lr + li * li
    f_re = (nr * lr + ab_im * li) / den
    f_im = (ab_im * lr - nr * li) / den
    bb_re = f_re[..., None] * b_re - f_im[..., None] * b_im
    bb_im = f_re[..., None] * b_im + f_im[..., None] * b_re
    gh = S5_GROUPS // S5_HALVES
    swap = lambda t: jnp.transpose(t, (0, 2, 1))
    halves = lambda t: [t[h * gh:(h + 1) * gh] for h in range(S5_HALVES)]
    bbd = jnp.stack([jnp.concatenate([_block_diag(swap(re)), _block_diag(swap(im))], axis=-1)
                     for re, im in zip(halves(bb_re), halves(bb_im))]).astype(BF16)
    cbd = jnp.stack([jnp.concatenate([_block_diag(swap(re)), -_block_diag(swap(im))], axis=0)
                     for re, im in zip(halves(c_re), halves(c_im))]).astype(BF16)
    flat = lambda t: [h.reshape(1, -1) for h in halves(t)]
    cat = lambda parts: jnp.concatenate(parts, axis=-1)
    a_plain = cat([cat([re, im]) for re, im in zip(flat(ab_re), flat(ab_im))])
    a_same = cat([cat([re, re]) for re in flat(ab_re)])
    a_cross = cat([cat([-im, im]) for im in flat(ab_im)])
    rep = lambda t: jnp.broadcast_to(t, (S5_SEGS, t.shape[-1]))
    return bbd, rep(a_same), rep(a_cross), a_plain, cbd


def _lora_weights(mu_wag, w1, a1, g1, mu_v, v1):
    d = w1.shape[0]
    pad = lambda t: jnp.pad(t, ((0, 0), (0, LORA_HALF - t.shape[1])))
    v1 = jnp.zeros((d, 32), F32) if v1 is None else v1
    mu_v = jnp.zeros((d,), F32) if mu_v is None else mu_v
    mats = (w1, a1, g1, v1)
    mus = (mu_wag[0], mu_wag[1], mu_wag[2], mu_v)
    keep = pad(jnp.concatenate([m * (1.0 - mu)[:, None] for m, mu in zip(mats, mus)], axis=1))
    shifted = pad(jnp.concatenate([m * mu[:, None] for m, mu in zip(mats, mus)], axis=1))
    return jnp.concatenate([keep, shifted], axis=1).astype(BF16)


def kernel(x, norm_mix, w_in, s5_lam_re, s5_lam_im, s5_log_dt, s5_b_re, s5_b_im, s5_c_re, s5_c_im, s5_d, s5_w_glu, s5_b_glu, rw_mu_rkv, rw_mu_wag, rw_w0, rw_w1, rw_w2, rw_a0, rw_a1, rw_a2, rw_g1, rw_g2, rw_k_k, rw_k_a, rw_r_k, rw_gn_w, rw_gn_b, rw_mu_v, rw_v0, rw_v1, rw_v2, lru_conv_w, lru_conv_b, lru_w_a, lru_b_a, lru_w_x, lru_b_x, lru_lam, w_branch, w_out, norm_ffn, w_ffn_gate, w_ffn_up, w_ffn_down, norm_final):
    b, s, d = x.shape
    depth = w_in.shape[0]
    w = WIDTH
    cos_t, sin_t = _rope_tables(s)
    head_id = jnp.arange(w) // HEAD_DIM
    hsum = (head_id[:, None] == head_id[None, :]).astype(BF16)
    expand = (jnp.arange(LANES)[:, None] == head_id[None, :]).astype(BF16)
    bf = lambda t: t.astype(BF16)
    row = lambda t: t.reshape(1, -1).astype(F32)

    v_first = None
    for l in range(depth):
        wl = w_in[l]
        cols = np.cumsum((0, w, w, w, w, w, w, w, w))
        seg = lambda i, j: bf(wl[:, cols[i]:cols[j]])
        has_vres = l > 0
        wlora = _lora_weights(rw_mu_wag[l], rw_w1[l], rw_a1[l], rw_g1[l],
                              rw_mu_v[l - 1] if has_vres else None, rw_v1[l - 1] if has_vres else None)
        lru_consts = (lru_conv_w[l].astype(F32), row(lru_conv_b[l]), bf(_block_diag(lru_w_a[l])), row(lru_b_a[l]),
                      bf(_block_diag(lru_w_x[l])), row(lru_b_x[l]), row(lru_lam[l]))
        h, u, rkv, y_d, lora, *qkv_views = _inproj(
            x, row(norm_mix[l]), cos_t, sin_t, seg(0, 2), seg(2, 3), seg(3, 4), seg(4, 7), seg(7, 8), wlora, lru_consts)

        att = [_attention(view, dil) for view, dil in zip(qkv_views, DILATIONS)]

        s5p = _s5_params(s5_lam_re[l], s5_lam_im[l], s5_log_dt[l], s5_b_re[l], s5_b_im[l], s5_c_re[l], s5_c_im[l])
        y_b = _s5(u, *s5p, row(s5_d[l]), bf(s5_w_glu[l]), row(s5_b_glu[l]))

        zero = jnp.zeros((w,), F32)
        vec = jnp.stack([rw_mu_rkv[l, 0], rw_mu_rkv[l, 1], rw_mu_rkv[l, 2], rw_w0[l], rw_a0[l],
                         rw_v0[l - 1] if has_vres else zero, rw_k_k[l], rw_k_a[l], rw_r_k[l].reshape(-1),
                         rw_gn_w[l], rw_gn_b[l]] + [zero] * 5).astype(F32)
        v2 = bf(jnp.pad(rw_v2[l - 1], ((0, 32), (0, 0)))) if has_vres else None
        y_c, v_c = _rwkv(rkv, lora, v_first, vec, bf(rw_w2[l]), bf(rw_a2[l]), bf(rw_g2[l]), v2, hsum)
        if l == 0:
            v_first = v_c

        x = _merge_ffn(x, h, [o for o, _ in att], [ls for _, ls in att], y_b, y_c, y_d,
                       bf(wl[:, cols[8]:]), bf(w_branch[l]), bf(w_out[l]), expand,
                       row(norm_ffn[l]), bf(w_ffn_gate[l]), bf(w_ffn_up[l]), bf(w_ffn_down[l]), row(norm_final),
                       final=(l == depth - 1))
    return x
```

```python
import functools
import math

import jax
import jax.numpy as jnp
import numpy as np
from jax import lax
from jax.experimental import pallas as pl
from jax.experimental.pallas import tpu as pltpu

F32 = jnp.float32
BF16 = jnp.bfloat16

D_MODEL = 1024
N_HEADS = 8
HEAD_DIM = 64
WIDTH = N_HEADS * HEAD_DIM
DILATIONS = (1, 4, 16)
ATT_BLOCK = 128
ROPE_THETA = 10000.0
S5_GROUPS = 32
S5_GROUP = 16
S5_STATE = 64
S5_NSTATE = S5_GROUPS * S5_STATE
RWKV_GN_EPS = 64e-5
RWKV_CHUNK = 64
LORA_HALF = 384
LRU_BLOCKS = 8
LRU_C = 8.0
D_FF = 2816
RMS_EPS = 1e-6
NEG_BIG = -1e30

LANES = 128
SUBLANES = 8
S5_SEGS = SUBLANES
S5_HALVES = 2
S5_HALF_STATES = S5_NSTATE // S5_HALVES
VMEM_LIMIT = 56 * 1024 * 1024

NT_DIMS = (((1,), (1,)), ((), ()))


def _dot(a, b):
    return jnp.dot(a, b, preferred_element_type=F32)


def _dot_nt(a, b):
    return lax.dot_general(a, b, NT_DIMS, preferred_element_type=F32)


def _split2(x):
    hi = x.astype(BF16)
    lo = (x - hi.astype(F32)).astype(BF16)
    return hi, lo


def _split3(x):
    hi = x.astype(BF16)
    r1 = x - hi.astype(F32)
    mid = r1.astype(BF16)
    lo = (r1 - mid.astype(F32)).astype(BF16)
    return hi, mid, lo


def _dot_sel(x, sel):
    hi, lo = _split2(x)
    return _dot(hi, sel) + _dot(lo, sel)


def _head_sums(xs, hsum):
    m = xs[0].shape[0]
    parts = []
    for x in xs:
        parts += list(_split2(x))
    stacked = jnp.concatenate(parts, axis=0)
    half = hsum.shape[0] // 2
    block = hsum[:half, :half]
    out = jnp.concatenate([_dot(stacked[:, :half], block), _dot(stacked[:, half:], block)], axis=-1)
    return [out[2 * j * m:(2 * j + 1) * m] + out[(2 * j + 1) * m:(2 * j + 2) * m] for j in range(len(xs))]


def _sigmoid(x):
    return 0.5 * jnp.tanh(0.5 * x) + 0.5


def _softplus(z):
    return jnp.maximum(z, 0.0) + jnp.log1p(jnp.exp(-jnp.abs(z)))


def _rms(x, g):
    ms = jnp.mean(x * x, axis=-1, keepdims=True)
    return x * lax.rsqrt(ms + RMS_EPS) * g


def _const_spec(shape):
    n = len(shape)
    return pl.BlockSpec(shape, lambda *_: (0,) * n, pipeline_mode=pl.Buffered(1))


def _params(sem):
    return pltpu.CompilerParams(dimension_semantics=sem, vmem_limit_bytes=VMEM_LIMIT)


def _seg_view_spec(s, tm):
    tiles_per_seg = s // S5_SEGS // tm
    return pl.BlockSpec((None, tm, WIDTH), lambda bi, i: (bi, i % tiles_per_seg, i // tiles_per_seg))


def _dilated_perm(tm, dil):
    dst = jnp.arange(tm)
    src = (dst % (tm // dil)) * dil + dst // (tm // dil)
    return (src[:, None] == jnp.arange(tm)[None, :]).astype(BF16)


def _lru_apply(x, cw_ref, cb_ref, wa_ref, ba_ref, wx_ref, bx_ref, lam_ref, o_ref, a_ref, carry_ref, h_ref):
    tb = x.shape[0]
    full = jnp.concatenate([carry_ref[...], x], axis=0)
    carry_ref[...] = x[tb - SUBLANES:]
    cw = cw_ref[...]
    taps = cw.shape[0]
    xc = cb_ref[...] + cw[taps - 1:taps, :] * x
    for j in range(1, taps):
        xc = xc + cw[taps - 1 - j:taps - j, :] * pltpu.roll(full, j, 0)[SUBLANES:]
    xcb = xc.astype(BF16)
    r = _sigmoid(_dot(xcb, wa_ref[...]) + ba_ref[...])
    i = _sigmoid(_dot(xcb, wx_ref[...]) + bx_ref[...])
    log_a = -LRU_C * r * _softplus(-lam_ref[...])
    a = jnp.exp(log_a)
    bb = jnp.sqrt(-jnp.tanh(log_a) * (a * a + 1.0)) * (i * xc)
    in_tile = lax.broadcasted_iota(jnp.int32, (tb, 1), 0) % SUBLANES
    for k in (1, 2, 4):
        keep = in_tile >= k
        bb = jnp.where(keep, a * pltpu.roll(bb, k, 0) + bb, bb)
        a = jnp.where(keep, a * pltpu.roll(a, k, 0), a)
    a_ref[...] = a
    o_ref[...] = bb
    h = h_ref[0:1, :]
    for n in range(tb // SUBLANES):
        rows = slice(n * SUBLANES, (n + 1) * SUBLANES)
        res = a_ref[rows, :] * h + o_ref[rows, :]
        o_ref[rows, :] = res
        h = res[SUBLANES - 1:SUBLANES, :]
    h_ref[0:1, :] = h


def _inproj_body(x_ref, g_ref, cos_ref, sin_ref, wqk_ref, wv_ref, wu_ref, wrkv_ref, wlru_ref, wlora_ref, *rest):
    n_dil = len(DILATIONS) - 1
    lru_consts, rest = rest[:7], rest[7:]
    perm_refs, (h_ref, u_ref, rkv_ref, lru_ref, lora_ref, qkv_ref), rest = rest[:n_dil], rest[n_dil:n_dil + 6], rest[n_dil + 6:]
    dil_refs, lru_scratch = rest[:n_dil], rest[n_dil:]
    tm = x_ref.shape[0]

    @pl.when(pl.program_id(1) == 0)
    def _():
        for ref in lru_scratch[1:]:
            ref[...] = jnp.zeros_like(ref)

    hb = _rms(x_ref[...], g_ref[...]).astype(BF16)
    h_ref[...] = hb
    _lru_apply(_dot(hb, wlru_ref[...]), *lru_consts, lru_ref, *lru_scratch)
    qk = _dot(hb, wqk_ref[...])
    reps = 2 * WIDTH // LANES
    cos = jnp.concatenate([cos_ref[...]] * reps, axis=-1)
    sin = jnp.concatenate([sin_ref[...]] * reps, axis=-1)
    lane = lax.broadcasted_iota(jnp.int32, qk.shape, 1)
    first_half = (lane % HEAD_DIM) < (HEAD_DIM // 2)
    partner = jnp.where(first_half,
                        pltpu.roll(qk, 2 * WIDTH - HEAD_DIM // 2, 1),
                        pltpu.roll(qk, HEAD_DIM // 2, 1))
    rot = qk * cos + partner * sin
    qkv = jnp.concatenate([(rot[:, :WIDTH] * (HEAD_DIM ** -0.5)).astype(BF16), rot[:, WIDTH:].astype(BF16),
                           _dot(hb, wv_ref[...]).astype(BF16)], axis=-1)
    qkv_ref[...] = qkv
    for dil, perm_ref, out_ref in zip(DILATIONS[1:], perm_refs, dil_refs):
        grouped = _dot(perm_ref[...], qkv).astype(BF16)
        rows = tm // dil
        for r in range(dil):
            out_ref[:, r * 3 * WIDTH:(r + 1) * 3 * WIDTH] = grouped[r * rows:(r + 1) * rows]
    u_ref[...] = _dot(hb, wu_ref[...])
    rkv_ref[...] = _dot(hb, wrkv_ref[...])
    lora_ref[...] = _dot(hb, wlora_ref[...])


def _inproj(x, g, cos, sin, wqk, wv, wu, wrkv, wlru, wlora, lru_consts, tm=256):
    b, s, d = x.shape
    grid = (b, s // tm)
    row = lambda w: pl.BlockSpec((None, tm, w), lambda bi, i: (bi, i, 0))
    tab = pl.BlockSpec((tm, LANES), lambda bi, i: (i, 0))
    widths = (d, WIDTH, 3 * WIDTH, WIDTH, 2 * LORA_HALF, 3 * WIDTH)
    dtypes = (BF16, F32, F32, F32, F32, BF16)
    out_specs = [row(w) for w in widths]
    out_shape = [jax.ShapeDtypeStruct((b, s, w), dt) for w, dt in zip(widths, dtypes)]
    out_specs[1] = _seg_view_spec(s, tm)
    out_shape[1] = jax.ShapeDtypeStruct((b, s // S5_SEGS, S5_SEGS * WIDTH), F32)
    for dil in DILATIONS[1:]:
        out_specs.append(pl.BlockSpec((None, tm // dil, dil * 3 * WIDTH), lambda bi, i: (bi, i, 0)))
        out_shape.append(jax.ShapeDtypeStruct((b, s // dil, dil * 3 * WIDTH), BF16))
    consts = (wqk, wv, wu, wrkv, wlru, wlora) + tuple(lru_consts) + tuple(_dilated_perm(tm, dil) for dil in DILATIONS[1:])
    return pl.pallas_call(
        _inproj_body,
        grid=grid,
        in_specs=[row(d), _const_spec((1, d)), tab, tab] + [_const_spec(c.shape) for c in consts],
        out_specs=out_specs,
        out_shape=out_shape,
        scratch_shapes=[pltpu.VMEM((tm, WIDTH), F32), pltpu.VMEM((SUBLANES, WIDTH), F32), pltpu.VMEM((SUBLANES, WIDTH), F32)],
        compiler_params=_params(("parallel", "arbitrary")),
        name="inproj",
    )(x, g, cos, sin, *consts)


def _attn_body(q_ref, kc_ref, kp_ref, vc_ref, o_ref, l_ref, vt_ref):
    i = pl.program_id(2)
    blk = ATT_BLOCK
    n_sub = q_ref.shape[0] // blk

    @pl.when(i == 0)
    def _():
        vt_ref[...] = jnp.zeros_like(vt_ref)

    rows = [slice(j * blk, (j + 1) * blk) for j in range(n_sub)]
    vts = [vt_ref[...]] + [vc_ref[r, :].astype(F32).T.astype(BF16) for r in rows]
    vt_ref[...] = vts[-1]
    key = lax.broadcasted_iota(jnp.int32, (2 * blk, blk), 0)
    qry = lax.broadcasted_iota(jnp.int32, (2 * blk, blk), 1)
    in_cur = jnp.logical_and(key >= blk, key - blk <= qry)
    in_prev = jnp.logical_and(key < blk, key >= qry)
    masks = [jnp.logical_or(in_cur, jnp.logical_and(in_prev, i > 0))] + [jnp.logical_or(in_cur, in_prev)] * (n_sub - 1)
    heads = [slice(h * HEAD_DIM, (h + 1) * HEAD_DIM) for h in range(N_HEADS)]
    pairs = [(j, sl) for j in range(n_sub) for sl in heads]
    prev_keys = lambda j, sl: kp_ref[:, sl] if j == 0 else kc_ref[rows[j - 1], sl]
    keys = [jnp.concatenate([prev_keys(j, sl), kc_ref[rows[j], sl]], axis=0) for j, sl in pairs]
    st = [jnp.where(masks[j], _dot_nt(k, q_ref[rows[j], sl]), NEG_BIG) for k, (j, sl) in zip(keys, pairs)]
    ms = [jnp.max(s, axis=0, keepdims=True) for s in st]
    ps = [jnp.exp(s - m) for s, m in zip(st, ms)]
    dens = [jnp.sum(p, axis=0, keepdims=True) for p in ps]
    outs = []
    for n, (j, sl) in enumerate(pairs):
        vt = jnp.concatenate([vts[j][sl, :], vts[j + 1][sl, :]], axis=1)
        outs.append(_dot(vt, ps[n].astype(BF16)) * (1.0 / dens[n]))
    pad = [jnp.zeros((LANES - N_HEADS, blk), F32)]
    for j in range(n_sub):
        mine = slice(j * N_HEADS, (j + 1) * N_HEADS)
        o_ref[rows[j], :] = jnp.concatenate(outs[mine], axis=0).T.astype(BF16)
        lse = jnp.concatenate([m + jnp.log(d) for m, d in zip(ms[mine], dens[mine])] + pad, axis=0)
        l_ref[rows[j], :] = lse.T


def _attention(qkv_view, dil):
    b, sub, _ = qkv_view.shape
    w = WIDTH
    n_blocks = sub // ATT_BLOCK
    n_sub = max(k for k in (4, 2, 1) if n_blocks % k == 0)
    step = n_sub * ATT_BLOCK
    cur = lambda j: pl.BlockSpec((None, step, w), lambda bi, r, i: (bi, i, 3 * r + j))
    prev_k = pl.BlockSpec((None, ATT_BLOCK, w), lambda bi, r, i: (bi, jnp.maximum(i * n_sub - 1, 0), 3 * r + 1))
    return pl.pallas_call(
        _attn_body,
        grid=(b, dil, n_blocks // n_sub),
        in_specs=[cur(0), cur(1), prev_k, cur(2)],
        out_specs=[pl.BlockSpec((None, step, w), lambda bi, r, i: (bi, i, r)),
                   pl.BlockSpec((None, step, LANES), lambda bi, r, i: (bi, i, r))],
        out_shape=[jax.ShapeDtypeStruct((b, sub, dil * w), BF16),
                   jax.ShapeDtypeStruct((b, sub, dil * LANES), F32)],
        scratch_shapes=[pltpu.VMEM((w, ATT_BLOCK), BF16)],
        compiler_params=_params(("parallel", "parallel", "arbitrary")),
        name=f"attn_d{dil}",
    )(*([qkv_view] * 4))


def _gelu_tanh(x):
    return 0.5 * x * (1.0 + jnp.tanh(math.sqrt(2.0 / math.pi) * (x + 0.044715 * (x * x * x))))


def _s5_slab(u_ref, perm_ref):
    u_stack = jnp.concatenate([u_ref[:, sg * WIDTH:(sg + 1) * WIDTH] for sg in range(S5_SEGS)], axis=0)
    u_slab = _dot(perm_ref[...], u_stack.astype(BF16)).astype(BF16)
    return u_stack, u_slab


def _s5_scan(u_slab, bbd_ref, asame_ref, across_ref, bu_ref, st_ref, store):
    hs = S5_HALF_STATES
    cin = WIDTH // S5_HALVES
    cols = [slice(half * 2 * hs, (half + 1) * 2 * hs) for half in range(S5_HALVES)]
    for half in range(S5_HALVES):
        bu_ref[half] = _dot(u_slab[:, half * cin:(half + 1) * cin], bbd_ref[half])
    xs = [st_ref[:, c] for c in cols]
    for t in range(bu_ref.shape[1] // S5_SEGS):
        rows = slice(t * S5_SEGS, (t + 1) * S5_SEGS)
        for half, c in enumerate(cols):
            x = xs[half]
            swapped = jnp.concatenate([x[:, hs:], x[:, :hs]], axis=-1)
            x = asame_ref[:, c] * x + across_ref[:, c] * swapped + bu_ref[half, rows, :]
            if store:
                bu_ref[half, rows, :] = x
            xs[half] = x
    for half, c in enumerate(cols):
        st_ref[:, c] = xs[half]


def _s5_state_body(u_ref, perm_ref, bbd_ref, asame_ref, across_ref, end_ref, bu_ref, st_ref):
    @pl.when(pl.program_id(1) == 0)
    def _():
        st_ref[...] = jnp.zeros_like(st_ref)

    _, u_slab = _s5_slab(u_ref, perm_ref)
    _s5_scan(u_slab, bbd_ref, asame_ref, across_ref, bu_ref, st_ref, store=False)

    @pl.when(pl.program_id(1) == pl.num_programs(1) - 1)
    def _():
        end_ref[...] = st_ref[...]


def _cmul(z, w):
    hs = z.shape[-1] // 2
    zr, zi, wr, wi = z[:, :hs], z[:, hs:], w[:, :hs], w[:, hs:]
    return jnp.concatenate([zr * wr - zi * wi, zr * wi + zi * wr], axis=-1)


def _s5_out_body(seg_len, u_ref, end_ref, perm_ref, permt_ref, bbd_ref, asame_ref, across_ref, aplain_ref,
                 cbd_ref, d_ref, wglu_ref, bglu_ref, o_ref, bu_ref, st_ref):
    hs = S5_HALF_STATES
    tt = u_ref.shape[0]

    @pl.when(pl.program_id(1) == 0)
    def _():
        for half in range(S5_HALVES):
            cols = slice(half * 2 * hs, (half + 1) * 2 * hs)
            base = aplain_ref[:, cols]
            power = None
            e = seg_len
            while e:
                if e & 1:
                    power = base if power is None else _cmul(power, base)
                e >>= 1
                if e:
                    base = _cmul(base, base)
            x = jnp.zeros((1, 2 * hs), F32)
            for sg in range(S5_SEGS):
                st_ref[sg:sg + 1, cols] = x
                x = _cmul(x, power) + end_ref[sg:sg + 1, cols]

    u_stack, u_slab = _s5_slab(u_ref, perm_ref)
    _s5_scan(u_slab, bbd_ref, asame_ref, across_ref, bu_ref, st_ref, store=True)
    ys = [_dot(bu_ref[half].astype(BF16), cbd_ref[half]) for half in range(S5_HALVES)]
    hi, lo = _split2(jnp.concatenate(ys, axis=-1))
    y = _dot(permt_ref[...], hi) + _dot(permt_ref[...], lo) + d_ref[...] * u_stack
    z = _dot(_gelu_tanh(y).astype(BF16), wglu_ref[...]) + bglu_ref[...]
    out = z[:, :WIDTH] * _sigmoid(z[:, WIDTH:])
    for sg in range(S5_SEGS):
        o_ref[:, sg * WIDTH:(sg + 1) * WIDTH] = out[sg * tt:(sg + 1) * tt]


def _s5(u_view, bbd, a_same, a_cross, a_plain, cbd, d_skip, wglu, bglu, tt=64):
    b, seg_len, wide = u_view.shape
    m = tt * S5_SEGS
    dst = jnp.arange(m)
    src = (dst % S5_SEGS) * tt + dst // S5_SEGS
    perm = (src[:, None] == jnp.arange(m)[None, :]).astype(BF16)
    row = pl.BlockSpec((None, tt, wide), lambda bi, i: (bi, i, 0))
    ends = pl.BlockSpec((None, S5_SEGS, 2 * S5_NSTATE), lambda bi, i: (bi, 0, 0))
    scratch = [pltpu.VMEM((S5_HALVES, m, 2 * S5_HALF_STATES), F32), pltpu.VMEM((S5_SEGS, 2 * S5_NSTATE), F32)]
    consts1 = (perm, bbd, a_same, a_cross)
    seg_end = pl.pallas_call(
        _s5_state_body,
        grid=(b, seg_len // tt),
        in_specs=[row] + [_const_spec(c.shape) for c in consts1],
        out_specs=ends,
        out_shape=jax.ShapeDtypeStruct((b, S5_SEGS, 2 * S5_NSTATE), F32),
        scratch_shapes=scratch,
        compiler_params=_params(("parallel", "arbitrary")),
        name="s5_state",
    )(u_view, *consts1)
    consts2 = (perm, perm.T, bbd, a_same, a_cross, a_plain, cbd, d_skip, wglu, bglu)
    return pl.pallas_call(
        functools.partial(_s5_out_body, seg_len),
        grid=(b, seg_len // tt),
        in_specs=[row, ends] + [_const_spec(c.shape) for c in consts2],
        out_specs=row,
        out_shape=jax.ShapeDtypeStruct((b, seg_len, wide), F32),
        scratch_shapes=scratch,
        compiler_params=_params(("parallel", "arbitrary")),
        name="s5_out",
    )(u_view, seg_end, *consts2)


V_MU_R, V_MU_K, V_MU_V, V_W0, V_A0, V_V0, V_KK, V_KA, V_RK, V_GNW, V_GNB = range(11)


def _rwkv_body(has_vres, *refs):
    if has_vres:
        (rkv_ref, lora_ref, vf_ref, vec_ref, w2_ref, a2_ref, g2_ref, v2_ref, hsum_ref,
         y_ref, st_ref, carry_ref, ysc_ref) = refs
    else:
        (rkv_ref, lora_ref, vec_ref, w2_ref, a2_ref, g2_ref, hsum_ref,
         y_ref, vout_ref, st_ref, carry_ref, ysc_ref) = refs
    L = RWKV_CHUNK
    nb, nj = rkv_ref.shape[0], rkv_ref.shape[1] // L
    m = nb * nj * L
    w = WIDTH

    @pl.when(pl.program_id(0) == 0)
    def _():
        st_ref[...] = jnp.zeros_like(st_ref)
        carry_ref[...] = jnp.zeros_like(carry_ref)

    vec = vec_ref[...]
    row = lambda j: vec[j:j + 1, :]
    hsum = hsum_ref[...]

    rkv = rkv_ref[...].reshape(m, 3 * w)
    lora = lora_ref[...].reshape(m, 2 * LORA_HALF)
    cat = jnp.concatenate([rkv, lora[:, LORA_HALF:]], axis=-1)
    row_id = lax.broadcasted_iota(jnp.int32, (m, 1), 0)
    prev = pltpu.roll(cat, 1, 0)
    for b in range(nb):
        prev = jnp.where(row_id == b * nj * L, carry_ref[b], prev)
        carry_ref[b] = cat[(b + 1) * nj * L - 1:(b + 1) * nj * L, :]
    lerp = lambda j, mu: rkv[:, j * w:(j + 1) * w] + (prev[:, j * w:(j + 1) * w] - rkv[:, j * w:(j + 1) * w]) * mu
    r = lerp(0, row(V_MU_R))
    k = lerp(1, row(V_MU_K))
    v = lerp(2, row(V_MU_V))
    pre = lora[:, :LORA_HALF] + prev[:, 3 * w:]

    wx = row(V_W0) + _dot(jnp.tanh(pre[:, 0:64]).astype(BF16), w2_ref[...])
    logw = -jnp.exp(-_softplus(-wx) - 0.5)
    a = _sigmoid(row(V_A0) + _dot(pre[:, 64:128].astype(BF16), a2_ref[...]))
    g = _dot(_sigmoid(pre[:, 128:256]).astype(BF16), g2_ref[...])
    if has_vres:
        vf = vf_ref[...].reshape(m, w)
        v = v + (vf - v) * _sigmoid(row(V_V0) + _dot(pre[:, 256:320].astype(BF16), v2_ref[...]))
    else:
        vout_ref[...] = v.reshape(nb, nj * L, w)

    kk = k * row(V_KK)
    k = k * (1.0 + (a - 1.0) * row(V_KA))
    kk_sq, rk_sum = _head_sums([kk * kk, r * k * row(V_RK)], hsum)
    kk = kk * lax.rsqrt(jnp.maximum(kk_sq, 1e-24))
    bonus = rk_sum * v

    pair = 2 * L
    mi = lax.broadcasted_iota(jnp.int32, (pair, pair), 0)
    mj = lax.broadcasted_iota(jnp.int32, (pair, pair), 1)
    same_chunk = (mi // L) == (mj // L)
    tri = jnp.where(jnp.logical_and(same_chunk, mj <= mi), 1.0, 0.0).astype(BF16)
    parts = _split3(logw)
    cl = jnp.concatenate([sum(_dot(tri, part[lo:lo + pair]) for part in parts) for lo in range(0, m, pair)], axis=0)
    e_neg = jnp.exp(-cl)
    a_t = ((-kk) * jnp.exp(cl - logw)).astype(BF16)
    b_t = ((kk * a) * e_neg).astype(BF16)
    k_t = (k * e_neg).astype(BF16)
    r_t = (r * jnp.exp(cl)).astype(BF16)
    v_b = v.astype(BF16)

    pw_lanes = 2 * HEAD_DIM
    lane = lax.broadcasted_iota(jnp.int32, (L, pw_lanes), 1)
    step = lax.broadcasted_iota(jnp.int32, (L, pw_lanes), 0)
    head0 = lane < HEAD_DIM
    upper_strict = step < lane % HEAD_DIM
    lower_incl = lane % HEAD_DIM <= step
    eye = (step == lane % HEAD_DIM).astype(F32)

    def per_head_rows(x):
        zero = jnp.zeros_like(x)
        return jnp.concatenate([jnp.where(head0, x, zero), jnp.where(head0, zero, x)], axis=0)

    chains = []
    for b in range(nb):
        for j in range(nj):
            lo = (b * nj + j) * L
            rows = slice(lo, lo + L)
            v_tt = v[rows].T.astype(BF16)
            p_end = jnp.exp(cl[lo + L - 1:lo + L, :])
            for p in range(N_HEADS // 2):
                sl = slice(p * pw_lanes, (p + 1) * pw_lanes)
                vtt = jnp.concatenate([v_tt[p * pw_lanes:p * pw_lanes + HEAD_DIM, :],
                                       v_tt[p * pw_lanes + HEAD_DIM:(p + 1) * pw_lanes, :]], axis=1)
                chains.append(dict(b=b, j=j, p=p, rows=rows, sl=sl, a_bd=per_head_rows(a_t[rows, sl]),
                                   b_bd=per_head_rows(b_t[rows, sl]), k_bd=per_head_rows(k_t[rows, sl]),
                                   bk=jnp.concatenate([b_t[rows, sl], k_t[rows, sl]], axis=0),
                                   r=r_t[rows, sl], v_bd=per_head_rows(v_b[rows, sl]), vtt=vtt, p_end=p_end[:, sl]))
    for c in chains:
        both = _dot_nt(c["bk"], c["a_bd"])
        c["tab"] = jnp.where(upper_strict, both[:L], 0.0)
        c["tak_bd"] = per_head_rows(jnp.where(upper_strict, both[L:], 0.0).astype(BF16))
        rbk = _dot_nt(c["r"], jnp.concatenate([c["b_bd"], c["k_bd"]], axis=0))
        c["trb"] = jnp.where(lower_incl, rbk[:, :pw_lanes], 0.0).astype(BF16)
        c["trk"] = jnp.where(lower_incl, rbk[:, pw_lanes:], 0.0).astype(BF16)
    t_idx = lane % HEAD_DIM
    coupling = lambda sz: jnp.logical_and(jnp.logical_and(step // (2 * sz) == t_idx // (2 * sz),
                                                         step // sz != t_idx // sz), step < t_idx)
    for c in chains:
        c["inv"] = eye + jnp.where(coupling(1), c["tab"], 0.0)
    sz = 2
    while sz < L:
        mask = coupling(sz)
        for c in chains:
            inv_bd = per_head_rows(c["inv"].astype(BF16))
            off_bd = per_head_rows(jnp.where(mask, c["tab"], 0.0).astype(BF16))
            c["inv"] = c["inv"] + _dot(_dot(c["inv"].astype(BF16), off_bd).astype(BF16), inv_bd)
        sz *= 2
    for c in chains:
        c["inv_bd"] = per_head_rows(c["inv"].astype(BF16))
    state = {(b, p): st_ref[b, p] for b in range(nb) for p in range(N_HEADS // 2)}
    for j in range(nj):
        group = [c for c in chains if c["j"] == j]
        for c in group:
            c["st"] = state[c["b"], c["p"]]
            c["stb"] = c["st"].astype(BF16)
            c["rhs"] = _dot_nt(c["stb"], c["a_bd"]) + _dot(c["vtt"], c["tak_bd"])
        for c in group:
            c["ut"] = _dot(c["rhs"].astype(BF16), c["inv_bd"]).astype(BF16)
        for c in group:
            state[c["b"], c["p"]] = (c["st"] + _dot(c["ut"], c["b_bd"]) + _dot(c["vtt"], c["k_bd"])) * c["p_end"]
        for c in group:
            lhs = jnp.concatenate([c["r"], c["trb"]], axis=1)
            rhs = jnp.concatenate([per_head_rows(c["stb"]), per_head_rows(c["ut"])], axis=1)
            ysc_ref[c["rows"], c["sl"]] = _dot_nt(lhs, rhs) + _dot(c["trk"], c["v_bd"])
    for (b, p), value in state.items():
        st_ref[b, p] = value

    y = ysc_ref[...]
    mean = _head_sums([y], hsum)[0] * (1.0 / HEAD_DIM)
    dev = y - mean
    var = _head_sums([dev * dev], hsum)[0] * (1.0 / HEAD_DIM)
    y = dev * lax.rsqrt(var + RWKV_GN_EPS) * row(V_GNW) + row(V_GNB)
    y_ref[...] = ((y + bonus) * g).reshape(nb, nj * L, w)


def _rwkv(rkv, lora, v_first, vec, w2, a2, g2, v2, hsum):
    b, s, _ = rkv.shape
    n_chunks = s // RWKV_CHUNK
    L = RWKV_CHUNK * max(k for k in (4, 2, 1) if n_chunks % k == 0)
    has_vres = v_first is not None
    row = lambda wd: pl.BlockSpec((b, L, wd), lambda i: (0, i, 0))
    ins = [rkv, lora] + ([v_first] if has_vres else []) + [vec, w2, a2, g2] + ([v2] if has_vres else []) + [hsum]
    in_specs = [row(3 * WIDTH), row(2 * LORA_HALF)] + ([row(WIDTH)] if has_vres else [])
    in_specs += [_const_spec(c.shape) for c in ins[len(in_specs):]]
    y_shape = jax.ShapeDtypeStruct((b, s, WIDTH), F32)
    out = pl.pallas_call(
        functools.partial(_rwkv_body, has_vres),
        grid=(s // L,),
        in_specs=in_specs,
        out_specs=row(WIDTH) if has_vres else [row(WIDTH), row(WIDTH)],
        out_shape=y_shape if has_vres else [y_shape, y_shape],
        scratch_shapes=[pltpu.VMEM((b, N_HEADS // 2, HEAD_DIM, 2 * HEAD_DIM), F32),
                        pltpu.VMEM((b, 1, 3 * WIDTH + LORA_HALF), F32),
                        pltpu.VMEM((b * L, WIDTH), F32)],
        compiler_params=_params(("arbitrary",)),
        name="rwkv_l1" if has_vres else "rwkv_l0",
    )(*ins)
    return (out, v_first) if has_vres else (out[0], out[1])


def _from_residue_view(ref, dil, unperm):
    if dil == 1:
        return ref[...].astype(F32)
    width = ref.shape[1] // dil
    stacked = jnp.concatenate([ref[:, r * width:(r + 1) * width] for r in range(dil)], axis=0)
    if stacked.dtype == BF16:
        return _dot(unperm, stacked)
    hi, lo = _split2(stacked)
    return _dot(unperm, hi) + _dot(unperm, lo)


def _merge_ffn_body(final, x_ref, h_ref, o1_ref, o2_ref, o3_ref, l1_ref, l2_ref, l3_ref, yb_ref, yc_ref, yd_ref,
                    wg_ref, wb_ref, wo_ref, exp_ref, up2_ref, up3_ref,
                    gn_ref, fg_ref, fu_ref, fd_ref, gf_ref, out_ref):
    hb = h_ref[...]
    unperms = (None, up2_ref[...], up3_ref[...])
    o1, o2, o3 = (_from_residue_view(r, d, p) for r, d, p in zip((o1_ref, o2_ref, o3_ref), DILATIONS, unperms))
    l1, l2, l3 = (_from_residue_view(r, d, p) for r, d, p in zip((l1_ref, l2_ref, l3_ref), DILATIONS, unperms))
    m = jnp.maximum(jnp.maximum(l1, l2), l3)
    e1, e2, e3 = jnp.exp(l1 - m), jnp.exp(l2 - m), jnp.exp(l3 - m)
    inv = 1.0 / (e1 + e2 + e3)
    expand = exp_ref[...]
    ya = (_dot_sel(e1 * inv, expand) * o1 + _dot_sel(e2 * inv, expand) * o2 + _dot_sel(e3 * inv, expand) * o3)
    merged = None
    for n, y in enumerate((ya, yb_ref[...], yc_ref[...], yd_ref[...])):
        gate = _sigmoid(_dot(hb, wg_ref[:, n * D_MODEL:(n + 1) * D_MODEL]))
        term = gate * _dot(y.astype(BF16), wb_ref[n])
        merged = term if merged is None else merged + term
    x = x_ref[...] + _dot(merged.astype(BF16), wo_ref[...])
    h2 = _rms(x, gn_ref[...]).astype(BF16)
    gate = _dot(h2, fg_ref[...])
    up = _dot(h2, fu_ref[...])
    y = x + _dot((gate * _sigmoid(gate) * up).astype(BF16), fd_ref[...])
    out_ref[...] = _rms(y, gf_ref[...]) if final else y


def _merge_ffn(x, h, os, ls, yb, yc, yd, wg, wb, wo, expand, gn, fg, fu, fd, gf, final, tm=256):
    b, s, d = x.shape
    row = lambda wd: pl.BlockSpec((None, tm, wd), lambda bi, i: (bi, i, 0))
    res = lambda wd: [pl.BlockSpec((None, tm // dil, dil * wd), lambda bi, i: (bi, i, 0)) for dil in DILATIONS]
    consts = (wg, wb, wo, expand) + tuple(_dilated_perm(tm, dil).T for dil in DILATIONS[1:]) + (gn, fg, fu, fd, gf)
    return pl.pallas_call(
        functools.partial(_merge_ffn_body, final),
        grid=(b, s // tm),
        in_specs=[row(d), row(d)] + res(WIDTH) + res(LANES)
        + [_seg_view_spec(s, tm), row(WIDTH), row(WIDTH)] + [_const_spec(c.shape) for c in consts],
        out_specs=row(d),
        out_shape=jax.ShapeDtypeStruct((b, s, d), F32),
        compiler_params=_params(("parallel", "parallel")),
        name="merge_ffn_final" if final else "merge_ffn",
    )(x, h, *os, *ls, yb, yc, yd, *consts)


def _rope_tables(s):
    inv = 1.0 / (ROPE_THETA ** (jnp.arange(0, HEAD_DIM, 2, dtype=F32) / HEAD_DIM))
    ang = jnp.arange(s, dtype=F32)[:, None] * inv[None, :]
    cos, sin = jnp.cos(ang), jnp.sin(ang)
    reps = LANES // HEAD_DIM
    cos_t = jnp.tile(jnp.concatenate([cos, cos], axis=-1), (1, reps))
    sin_t = jnp.tile(jnp.concatenate([-sin, sin], axis=-1), (1, reps))
    return cos_t, sin_t


def _block_diag(blocks):
    g, r, c = blocks.shape
    eye = jnp.eye(g, dtype=blocks.dtype)
    return (eye[:, None, :, None] * blocks[:, :, None, :]).reshape(g * r, g * c)


def _s5_params(lam_re, lam_im, log_dt, b_re, b_im, c_re, c_im):
    lr, li = lam_re.astype(F32), lam_im.astype(F32)
    dt = jnp.exp(log_dt.astype(F32))[:, None]
    mag = jnp.exp(lr * dt)
    ab_re, ab_im = mag * jnp.cos(li * dt), mag * jnp.sin(li * dt)
    nr = ab_re - 1.0
    den = lr * lr + li * li
    f_re = (nr * lr + ab_im * li) / den
    f_im = (ab_im * lr - nr * li) / den
    bb_re = f_re[..., None] * b_re - f_im[..., None] * b_im
    bb_im = f_re[..., None] * b_im + f_im[..., None] * b_re
    gh = S5_GROUPS // S5_HALVES
    swap = lambda t: jnp.transpose(t, (0, 2, 1))
    halves = lambda t: [t[h * gh:(h + 1) * gh] for h in range(S5_HALVES)]
    bbd = jnp.stack([jnp.concatenate([_block_diag(swap(re)), _block_diag(swap(im))], axis=-1)
                     for re, im in zip(halves(bb_re), halves(bb_im))]).astype(BF16)
    cbd = jnp.stack([jnp.concatenate([_block_diag(swap(re)), -_block_diag(swap(im))], axis=0)
                     for re, im in zip(halves(c_re), halves(c_im))]).astype(BF16)
    flat = lambda t: [h.reshape(1, -1) for h in halves(t)]
    cat = lambda parts: jnp.concatenate(parts, axis=-1)
    a_plain = cat([cat([re, im]) for re, im in zip(flat(ab_re), flat(ab_im))])
    a_same = cat([cat([re, re]) for re in flat(ab_re)])
    a_cross = cat([cat([-im, im]) for im in flat(ab_im)])
    rep = lambda t: jnp.broadcast_to(t, (S5_SEGS, t.shape[-1]))
    return bbd, rep(a_same), rep(a_cross), a_plain, cbd


def _lora_weights(mu_wag, w1, a1, g1, mu_v, v1):
    d = w1.shape[0]
    pad = lambda t: jnp.pad(t, ((0, 0), (0, LORA_HALF - t.shape[1])))
    v1 = jnp.zeros((d, 32), F32) if v1 is None else v1
    mu_v = jnp.zeros((d,), F32) if mu_v is None else mu_v
    mats = (w1, a1, g1, v1)
    mus = (mu_wag[0], mu_wag[1], mu_wag[2], mu_v)
    keep = pad(jnp.concatenate([m * (1.0 - mu)[:, None] for m, mu in zip(mats, mus)], axis=1))
    shifted = pad(jnp.concatenate([m * mu[:, None] for m, mu in zip(mats, mus)], axis=1))
    return jnp.concatenate([keep, shifted], axis=1).astype(BF16)


def kernel(x, norm_mix, w_in, s5_lam_re, s5_lam_im, s5_log_dt, s5_b_re, s5_b_im, s5_c_re, s5_c_im, s5_d, s5_w_glu, s5_b_glu, rw_mu_rkv, rw_mu_wag, rw_w0, rw_w1, rw_w2, rw_a0, rw_a1, rw_a2, rw_g1, rw_g2, rw_k_k, rw_k_a, rw_r_k, rw_gn_w, rw_gn_b, rw_mu_v, rw_v0, rw_v1, rw_v2, lru_conv_w, lru_conv_b, lru_w_a, lru_b_a, lru_w_x, lru_b_x, lru_lam, w_branch, w_out, norm_ffn, w_ffn_gate, w_ffn_up, w_ffn_down, norm_final):
    b, s, d = x.shape
    depth = w_in.shape[0]
    w = WIDTH
    cos_t, sin_t = _rope_tables(s)
    head_id = jnp.arange(w) // HEAD_DIM
    hsum = (head_id[:, None] == head_id[None, :]).astype(BF16)
    expand = (jnp.arange(LANES)[:, None] == head_id[None, :]).astype(BF16)
    bf = lambda t: t.astype(BF16)
    row = lambda t: t.reshape(1, -1).astype(F32)

    v_first = None
    for l in range(depth):
        wl = w_in[l]
        cols = np.cumsum((0, w, w, w, w, w, w, w, w))
        seg = lambda i, j: bf(wl[:, cols[i]:cols[j]])
        has_vres = l > 0
        wlora = _lora_weights(rw_mu_wag[l], rw_w1[l], rw_a1[l], rw_g1[l],
                              rw_mu_v[l - 1] if has_vres else None, rw_v1[l - 1] if has_vres else None)
        lru_consts = (lru_conv_w[l].astype(F32), row(lru_conv_b[l]), bf(_block_diag(lru_w_a[l])), row(lru_b_a[l]),
                      bf(_block_diag(lru_w_x[l])), row(lru_b_x[l]), row(lru_lam[l]))
        h, u, rkv, y_d, lora, *qkv_views = _inproj(
            x, row(norm_mix[l]), cos_t, sin_t, seg(0, 2), seg(2, 3), seg(3, 4), seg(4, 7), seg(7, 8), wlora, lru_consts)

        att = [_attention(view, dil) for view, dil in zip(qkv_views, DILATIONS)]

        s5p = _s5_params(s5_lam_re[l], s5_lam_im[l], s5_log_dt[l], s5_b_re[l], s5_b_im[l], s5_c_re[l], s5_c_im[l])
        y_b = _s5(u, *s5p, row(s5_d[l]), bf(s5_w_glu[l]), row(s5_b_glu[l]))

        zero = jnp.zeros((w,), F32)
        vec = jnp.stack([rw_mu_rkv[l, 0], rw_mu_rkv[l, 1], rw_mu_rkv[l, 2], rw_w0[l], rw_a0[l],
                         rw_v0[l - 1] if has_vres else zero, rw_k_k[l], rw_k_a[l], rw_r_k[l].reshape(-1),
                         rw_gn_w[l], rw_gn_b[l]] + [zero] * 5).astype(F32)
        v2 = bf(jnp.pad(rw_v2[l - 1], ((0, 32), (0, 0)))) if has_vres else None
        y_c, v_c = _rwkv(rkv, lora, v_first, vec, bf(rw_w2[l]), bf(rw_a2[l]), bf(rw_g2[l]), v2, hsum)
        if l == 0:
            v_first = v_c

        x = _merge_ffn(x, h, [o for o, _ in att], [ls for _, ls in att], y_b, y_c, y_d,
                       bf(wl[:, cols[8]:]), bf(w_branch[l]), bf(w_out[l]), expand,
                       row(norm_ffn[l]), bf(w_ffn_gate[l]), bf(w_ffn_up[l]), bf(w_ffn_down[l]), row(norm_final),
                       final=(l == depth - 1))
    return x
```

```python
import functools
import math

import jax
import jax.numpy as jnp
import numpy as np
from jax import lax
from jax.experimental import pallas as pl
from jax.experimental.pallas import tpu as pltpu

F32 = jnp.float32
BF16 = jnp.bfloat16

D_MODEL = 1024
N_HEADS = 8
HEAD_DIM = 64
WIDTH = N_HEADS * HEAD_DIM
DILATIONS = (1, 4, 16)
ATT_BLOCK = 128
ROPE_THETA = 10000.0
S5_GROUPS = 32
S5_GROUP = 16
S5_STATE = 64
S5_NSTATE = S5_GROUPS * S5_STATE
RWKV_GN_EPS = 64e-5
RWKV_CHUNK = 64
LORA_HALF = 384
LRU_BLOCKS = 8
LRU_C = 8.0
D_FF = 2816
RMS_EPS = 1e-6
NEG_BIG = -1e30
Q_SCALE = math.log2(math.e) * HEAD_DIM ** -0.5

LANES = 128
SUBLANES = 8
S5_SEGS = SUBLANES
S5_HALVES = 2
S5_HALF_STATES = S5_NSTATE // S5_HALVES
VMEM_LIMIT = 56 * 1024 * 1024

NT_DIMS = (((1,), (1,)), ((), ()))


def _dot(a, b):
    return jnp.dot(a, b, preferred_element_type=F32)


def _dot_nt(a, b):
    return lax.dot_general(a, b, NT_DIMS, preferred_element_type=F32)


def _split2(x):
    hi = x.astype(BF16)
    lo = (x - hi.astype(F32)).astype(BF16)
    return hi, lo


def _split3(x):
    hi = x.astype(BF16)
    r1 = x - hi.astype(F32)
    mid = r1.astype(BF16)
    lo = (r1 - mid.astype(F32)).astype(BF16)
    return hi, mid, lo


def _dot_sel(x, sel):
    hi, lo = _split2(x)
    return _dot(hi, sel) + _dot(lo, sel)


def _head_sums(xs, hsum):
    m = xs[0].shape[0]
    parts = []
    for x in xs:
        parts += list(_split2(x))
    stacked = jnp.concatenate(parts, axis=0)
    half = hsum.shape[0] // 2
    block = hsum[:half, :half]
    out = jnp.concatenate([_dot(stacked[:, :half], block), _dot(stacked[:, half:], block)], axis=-1)
    return [out[2 * j * m:(2 * j + 1) * m] + out[(2 * j + 1) * m:(2 * j + 2) * m] for j in range(len(xs))]


def _sigmoid(x):
    return 0.5 * jnp.tanh(0.5 * x) + 0.5


def _softplus(z):
    return jnp.maximum(z, 0.0) + jnp.log1p(jnp.exp(-jnp.abs(z)))


def _rms(x, g):
    ms = jnp.mean(x * x, axis=-1, keepdims=True)
    return x * lax.rsqrt(ms + RMS_EPS) * g


def _const_spec(shape):
    n = len(shape)
    return pl.BlockSpec(shape, lambda *_: (0,) * n, pipeline_mode=pl.Buffered(1))


def _params(sem):
    return pltpu.CompilerParams(dimension_semantics=sem, vmem_limit_bytes=VMEM_LIMIT)


def _seg_view_spec(s, tm):
    tiles_per_seg = s // S5_SEGS // tm
    return pl.BlockSpec((None, tm, WIDTH), lambda bi, i: (bi, i % tiles_per_seg, i // tiles_per_seg))


def _dilated_perm(tm, dil):
    dst = jnp.arange(tm)
    src = (dst % (tm // dil)) * dil + dst // (tm // dil)
    return (src[:, None] == jnp.arange(tm)[None, :]).astype(BF16)


def _lru_apply(x, cw_ref, cb_ref, wa_ref, ba_ref, wx_ref, bx_ref, lam_ref, o_ref, a_ref, carry_ref, h_ref):
    tb = x.shape[0]
    full = jnp.concatenate([carry_ref[...], x], axis=0)
    carry_ref[...] = x[tb - SUBLANES:]
    cw = cw_ref[...]
    taps = cw.shape[0]
    xc = cb_ref[...] + cw[taps - 1:taps, :] * x
    for j in range(1, taps):
        xc = xc + cw[taps - 1 - j:taps - j, :] * pltpu.roll(full, j, 0)[SUBLANES:]
    xcb = xc.astype(BF16)
    r = _sigmoid(_dot(xcb, wa_ref[...]) + ba_ref[...])
    i = _sigmoid(_dot(xcb, wx_ref[...]) + bx_ref[...])
    log_a = -LRU_C * r * _softplus(-lam_ref[...])
    a = jnp.exp(log_a)
    bb = jnp.sqrt(-jnp.tanh(log_a) * (a * a + 1.0)) * (i * xc)
    in_tile = lax.broadcasted_iota(jnp.int32, (tb, 1), 0) % SUBLANES
    for k in (1, 2, 4):
        keep = in_tile >= k
        bb = jnp.where(keep, a * pltpu.roll(bb, k, 0) + bb, bb)
        a = jnp.where(keep, a * pltpu.roll(a, k, 0), a)
    a_ref[...] = a
    o_ref[...] = bb
    h = h_ref[0:1, :]
    for n in range(tb // SUBLANES):
        rows = slice(n * SUBLANES, (n + 1) * SUBLANES)
        res = a_ref[rows, :] * h + o_ref[rows, :]
        o_ref[rows, :] = res
        h = res[SUBLANES - 1:SUBLANES, :]
    h_ref[0:1, :] = h


def _inproj_body(x_ref, g_ref, cos_ref, sin_ref, wqk_ref, wv_ref, wu_ref, wrkv_ref, wlru_ref, wlora_ref, *rest):
    n_dil = len(DILATIONS) - 1
    lru_consts, rest = rest[:7], rest[7:]
    perm_refs, (h_ref, u_ref, rkv_ref, lru_ref, lora_ref, qkv_ref), rest = rest[:n_dil], rest[n_dil:n_dil + 6], rest[n_dil + 6:]
    dil_refs, lru_scratch = rest[:n_dil], rest[n_dil:]
    tm = x_ref.shape[0]

    @pl.when(pl.program_id(1) == 0)
    def _():
        for ref in lru_scratch[1:]:
            ref[...] = jnp.zeros_like(ref)

    hb = _rms(x_ref[...], g_ref[...]).astype(BF16)
    h_ref[...] = hb
    _lru_apply(_dot(hb, wlru_ref[...]), *lru_consts, lru_ref, *lru_scratch)
    qk = _dot(hb, wqk_ref[...])
    reps = 2 * WIDTH // LANES
    cos = jnp.concatenate([cos_ref[...]] * reps, axis=-1)
    sin = jnp.concatenate([sin_ref[...]] * reps, axis=-1)
    lane = lax.broadcasted_iota(jnp.int32, qk.shape, 1)
    first_half = (lane % HEAD_DIM) < (HEAD_DIM // 2)
    partner = jnp.where(first_half,
                        pltpu.roll(qk, 2 * WIDTH - HEAD_DIM // 2, 1),
                        pltpu.roll(qk, HEAD_DIM // 2, 1))
    rot = qk * cos + partner * sin
    qkv = jnp.concatenate([(rot[:, :WIDTH] * Q_SCALE).astype(BF16), rot[:, WIDTH:].astype(BF16),
                           _dot(hb, wv_ref[...]).astype(BF16)], axis=-1)
    qkv_ref[...] = qkv
    for dil, perm_ref, out_ref in zip(DILATIONS[1:], perm_refs, dil_refs):
        grouped = _dot(perm_ref[...], qkv).astype(BF16)
        rows = tm // dil
        for r in range(dil):
            out_ref[:, r * 3 * WIDTH:(r + 1) * 3 * WIDTH] = grouped[r * rows:(r + 1) * rows]
    u_ref[...] = _dot(hb, wu_ref[...])
    rkv_ref[...] = _dot(hb, wrkv_ref[...])
    lora_ref[...] = _dot(hb, wlora_ref[...])


def _inproj(x, g, cos, sin, wqk, wv, wu, wrkv, wlru, wlora, lru_consts, tm=256):
    b, s, d = x.shape
    grid = (b, s // tm)
    row = lambda w: pl.BlockSpec((None, tm, w), lambda bi, i: (bi, i, 0))
    tab = pl.BlockSpec((tm, LANES), lambda bi, i: (i, 0))
    widths = (d, WIDTH, 3 * WIDTH, WIDTH, 2 * LORA_HALF, 3 * WIDTH)
    dtypes = (BF16, F32, F32, F32, F32, BF16)
    out_specs = [row(w) for w in widths]
    out_shape = [jax.ShapeDtypeStruct((b, s, w), dt) for w, dt in zip(widths, dtypes)]
    out_specs[1] = _seg_view_spec(s, tm)
    out_shape[1] = jax.ShapeDtypeStruct((b, s // S5_SEGS, S5_SEGS * WIDTH), F32)
    for dil in DILATIONS[1:]:
        out_specs.append(pl.BlockSpec((None, tm // dil, dil * 3 * WIDTH), lambda bi, i: (bi, i, 0)))
        out_shape.append(jax.ShapeDtypeStruct((b, s // dil, dil * 3 * WIDTH), BF16))
    consts = (wqk, wv, wu, wrkv, wlru, wlora) + tuple(lru_consts) + tuple(_dilated_perm(tm, dil) for dil in DILATIONS[1:])
    return pl.pallas_call(
        _inproj_body,
        grid=grid,
        in_specs=[row(d), _const_spec((1, d)), tab, tab] + [_const_spec(c.shape) for c in consts],
        out_specs=out_specs,
        out_shape=out_shape,
        scratch_shapes=[pltpu.VMEM((tm, WIDTH), F32), pltpu.VMEM((SUBLANES, WIDTH), F32), pltpu.VMEM((SUBLANES, WIDTH), F32)],
        compiler_params=_params(("parallel", "arbitrary")),
        name="inproj",
    )(x, g, cos, sin, *consts)


def _attn_body(q_ref, kc_ref, kp_ref, vc_ref, o_ref, l_ref, vt_ref):
    i = pl.program_id(2)
    blk = ATT_BLOCK
    n_sub = q_ref.shape[0] // blk

    @pl.when(i == 0)
    def _():
        vt_ref[...] = jnp.zeros_like(vt_ref)

    rows = [slice(j * blk, (j + 1) * blk) for j in range(n_sub)]
    vts = [vt_ref[...]] + [vc_ref[r, :].astype(F32).T.astype(BF16) for r in rows]
    vt_ref[...] = vts[-1]
    key = lax.broadcasted_iota(jnp.int32, (2 * blk, blk), 0)
    qry = lax.broadcasted_iota(jnp.int32, (2 * blk, blk), 1)
    in_cur = jnp.logical_and(key >= blk, key - blk <= qry)
    in_prev = jnp.logical_and(key < blk, key >= qry)
    masks = [jnp.logical_or(in_cur, jnp.logical_and(in_prev, i > 0))] + [jnp.logical_or(in_cur, in_prev)] * (n_sub - 1)
    heads = [slice(h * HEAD_DIM, (h + 1) * HEAD_DIM) for h in range(N_HEADS)]
    pairs = [(j, sl) for j in range(n_sub) for sl in heads]
    prev_keys = lambda j, sl: kp_ref[:, sl] if j == 0 else kc_ref[rows[j - 1], sl]
    keys = [jnp.concatenate([prev_keys(j, sl), kc_ref[rows[j], sl]], axis=0) for j, sl in pairs]
    st = [jnp.where(masks[j], _dot_nt(k, q_ref[rows[j], sl]), NEG_BIG) for k, (j, sl) in zip(keys, pairs)]
    ms = [jnp.max(s, axis=0, keepdims=True) for s in st]
    ps = [jnp.exp2(s - m).astype(BF16) for s, m in zip(st, ms)]
    ones = jnp.ones((2 * SUBLANES, 2 * blk), BF16)
    outs, dens = [], []
    for n, (j, sl) in enumerate(pairs):
        vt = jnp.concatenate([vts[j][sl, :], vts[j + 1][sl, :]], axis=1)
        res = _dot(jnp.concatenate([vt, ones], axis=0), ps[n])
        dens.append(res[HEAD_DIM:HEAD_DIM + 1, :])
        outs.append(res[:HEAD_DIM, :] * (1.0 / dens[n]))
    pad = [jnp.zeros((LANES - N_HEADS, blk), F32)]
    for j in range(n_sub):
        mine = slice(j * N_HEADS, (j + 1) * N_HEADS)
        o_ref[rows[j], :] = jnp.concatenate(outs[mine], axis=0).T.astype(BF16)
        lse = jnp.concatenate([(m + jnp.log2(d)) * math.log(2.0) for m, d in zip(ms[mine], dens[mine])] + pad, axis=0)
        l_ref[rows[j], :] = lse.T


def _attention(qkv_view, dil):
    b, sub, _ = qkv_view.shape
    w = WIDTH
    n_blocks = sub // ATT_BLOCK
    n_sub = max(k for k in (4, 2, 1) if n_blocks % k == 0)
    step = n_sub * ATT_BLOCK
    cur = lambda j: pl.BlockSpec((None, step, w), lambda bi, r, i: (bi, i, 3 * r + j))
    prev_k = pl.BlockSpec((None, ATT_BLOCK, w), lambda bi, r, i: (bi, jnp.maximum(i * n_sub - 1, 0), 3 * r + 1))
    return pl.pallas_call(
        _attn_body,
        grid=(b, dil, n_blocks // n_sub),
        in_specs=[cur(0), cur(1), prev_k, cur(2)],
        out_specs=[pl.BlockSpec((None, step, w), lambda bi, r, i: (bi, i, r)),
                   pl.BlockSpec((None, step, LANES), lambda bi, r, i: (bi, i, r))],
        out_shape=[jax.ShapeDtypeStruct((b, sub, dil * w), BF16),
                   jax.ShapeDtypeStruct((b, sub, dil * LANES), F32)],
        scratch_shapes=[pltpu.VMEM((w, ATT_BLOCK), BF16)],
        compiler_params=_params(("parallel", "parallel", "arbitrary")),
        name=f"attn_d{dil}",
    )(*([qkv_view] * 4))


def _gelu_tanh(x):
    return 0.5 * x * (1.0 + jnp.tanh(math.sqrt(2.0 / math.pi) * (x + 0.044715 * (x * x * x))))


def _s5_slab(u_ref, perm_ref):
    u_stack = jnp.concatenate([u_ref[:, sg * WIDTH:(sg + 1) * WIDTH] for sg in range(S5_SEGS)], axis=0)
    u_slab = _dot(perm_ref[...], u_stack.astype(BF16)).astype(BF16)
    return u_stack, u_slab


def _s5_scan(u_slab, bbd_ref, asame_ref, across_ref, bu_ref, st_ref, store):
    hs = S5_HALF_STATES
    cin = WIDTH // S5_HALVES
    cols = [slice(half * 2 * hs, (half + 1) * 2 * hs) for half in range(S5_HALVES)]
    for half in range(S5_HALVES):
        bu_ref[half] = _dot(u_slab[:, half * cin:(half + 1) * cin], bbd_ref[half])
    xs = [st_ref[:, c] for c in cols]
    for t in range(bu_ref.shape[1] // S5_SEGS):
        rows = slice(t * S5_SEGS, (t + 1) * S5_SEGS)
        for half, c in enumerate(cols):
            x = xs[half]
            swapped = jnp.concatenate([x[:, hs:], x[:, :hs]], axis=-1)
            x = asame_ref[:, c] * x + across_ref[:, c] * swapped + bu_ref[half, rows, :]
            if store:
                bu_ref[half, rows, :] = x
            xs[half] = x
    for half, c in enumerate(cols):
        st_ref[:, c] = xs[half]


def _s5_state_body(u_ref, perm_ref, bbd_ref, asame_ref, across_ref, end_ref, bu_ref, st_ref):
    @pl.when(pl.program_id(1) == 0)
    def _():
        st_ref[...] = jnp.zeros_like(st_ref)

    _, u_slab = _s5_slab(u_ref, perm_ref)
    _s5_scan(u_slab, bbd_ref, asame_ref, across_ref, bu_ref, st_ref, store=False)

    @pl.when(pl.program_id(1) == pl.num_programs(1) - 1)
    def _():
        end_ref[...] = st_ref[...]


def _cmul(z, w):
    hs = z.shape[-1] // 2
    zr, zi, wr, wi = z[:, :hs], z[:, hs:], w[:, :hs], w[:, hs:]
    return jnp.concatenate([zr * wr - zi * wi, zr * wi + zi * wr], axis=-1)


def _s5_out_body(seg_len, u_ref, end_ref, perm_ref, permt_ref, bbd_ref, asame_ref, across_ref, aplain_ref,
                 cbd_ref, d_ref, wglu_ref, bglu_ref, o_ref, bu_ref, st_ref):
    hs = S5_HALF_STATES
    tt = u_ref.shape[0]

    @pl.when(pl.program_id(1) == 0)
    def _():
        for half in range(S5_HALVES):
            cols = slice(half * 2 * hs, (half + 1) * 2 * hs)
            base = aplain_ref[:, cols]
            power = None
            e = seg_len
            while e:
                if e & 1:
                    power = base if power is None else _cmul(power, base)
                e >>= 1
                if e:
                    base = _cmul(base, base)
            x = jnp.zeros((1, 2 * hs), F32)
            for sg in range(S5_SEGS):
                st_ref[sg:sg + 1, cols] = x
                x = _cmul(x, power) + end_ref[sg:sg + 1, cols]

    u_stack, u_slab = _s5_slab(u_ref, perm_ref)
    _s5_scan(u_slab, bbd_ref, asame_ref, across_ref, bu_ref, st_ref, store=True)
    ys = [_dot(bu_ref[half].astype(BF16), cbd_ref[half]) for half in range(S5_HALVES)]
    hi, lo = _split2(jnp.concatenate(ys, axis=-1))
    y = _dot(permt_ref[...], hi) + _dot(permt_ref[...], lo) + d_ref[...] * u_stack
    z = _dot(_gelu_tanh(y).astype(BF16), wglu_ref[...]) + bglu_ref[...]
    out = z[:, :WIDTH] * _sigmoid(z[:, WIDTH:])
    for sg in range(S5_SEGS):
        o_ref[:, sg * WIDTH:(sg + 1) * WIDTH] = out[sg * tt:(sg + 1) * tt]


def _s5(u_view, bbd, a_same, a_cross, a_plain, cbd, d_skip, wglu, bglu, tt=64):
    b, seg_len, wide = u_view.shape
    m = tt * S5_SEGS
    dst = jnp.arange(m)
    src = (dst % S5_SEGS) * tt + dst // S5_SEGS
    perm = (src[:, None] == jnp.arange(m)[None, :]).astype(BF16)
    row = pl.BlockSpec((None, tt, wide), lambda bi, i: (bi, i, 0))
    ends = pl.BlockSpec((None, S5_SEGS, 2 * S5_NSTATE), lambda bi, i: (bi, 0, 0))
    scratch = [pltpu.VMEM((S5_HALVES, m, 2 * S5_HALF_STATES), F32), pltpu.VMEM((S5_SEGS, 2 * S5_NSTATE), F32)]
    consts1 = (perm, bbd, a_same, a_cross)
    seg_end = pl.pallas_call(
        _s5_state_body,
        grid=(b, seg_len // tt),
        in_specs=[row] + [_const_spec(c.shape) for c in consts1],
        out_specs=ends,
        out_shape=jax.ShapeDtypeStruct((b, S5_SEGS, 2 * S5_NSTATE), F32),
        scratch_shapes=scratch,
        compiler_params=_params(("parallel", "arbitrary")),
        name="s5_state",
    )(u_view, *consts1)
    consts2 = (perm, perm.T, bbd, a_same, a_cross, a_plain, cbd, d_skip, wglu, bglu)
    return pl.pallas_call(
        functools.partial(_s5_out_body, seg_len),
        grid=(b, seg_len // tt),
        in_specs=[row, ends] + [_const_spec(c.shape) for c in consts2],
        out_specs=row,
        out_shape=jax.ShapeDtypeStruct((b, seg_len, wide), F32),
        scratch_shapes=scratch,
        compiler_params=_params(("parallel", "arbitrary")),
        name="s5_out",
    )(u_view, seg_end, *consts2)


V_MU_R, V_MU_K, V_MU_V, V_W0, V_A0, V_V0, V_KK, V_KA, V_RK, V_GNW, V_GNB = range(11)


def _rwkv_body(has_vres, *refs):
    if has_vres:
        (rkv_ref, lora_ref, vf_ref, vec_ref, w2_ref, a2_ref, g2_ref, v2_ref, hsum_ref,
         y_ref, st_ref, carry_ref, ysc_ref) = refs
    else:
        (rkv_ref, lora_ref, vec_ref, w2_ref, a2_ref, g2_ref, hsum_ref,
         y_ref, vout_ref, st_ref, carry_ref, ysc_ref) = refs
    L = RWKV_CHUNK
    nb, nj = rkv_ref.shape[0], rkv_ref.shape[1] // L
    m = nb * nj * L
    w = WIDTH

    @pl.when(pl.program_id(0) == 0)
    def _():
        st_ref[...] = jnp.zeros_like(st_ref)
        carry_ref[...] = jnp.zeros_like(carry_ref)

    vec = vec_ref[...]
    row = lambda j: vec[j:j + 1, :]
    hsum = hsum_ref[...]

    rkv = rkv_ref[...].reshape(m, 3 * w)
    lora = lora_ref[...].reshape(m, 2 * LORA_HALF)
    cat = jnp.concatenate([rkv, lora[:, LORA_HALF:]], axis=-1)
    row_id = lax.broadcasted_iota(jnp.int32, (m, 1), 0)
    prev = pltpu.roll(cat, 1, 0)
    for b in range(nb):
        prev = jnp.where(row_id == b * nj * L, carry_ref[b], prev)
        carry_ref[b] = cat[(b + 1) * nj * L - 1:(b + 1) * nj * L, :]
    lerp = lambda j, mu: rkv[:, j * w:(j + 1) * w] + (prev[:, j * w:(j + 1) * w] - rkv[:, j * w:(j + 1) * w]) * mu
    r = lerp(0, row(V_MU_R))
    k = lerp(1, row(V_MU_K))
    v = lerp(2, row(V_MU_V))
    pre = lora[:, :LORA_HALF] + prev[:, 3 * w:]

    wx = row(V_W0) + _dot(jnp.tanh(pre[:, 0:64]).astype(BF16), w2_ref[...])
    logw = -jnp.exp(-_softplus(-wx) - 0.5)
    a = _sigmoid(row(V_A0) + _dot(pre[:, 64:128].astype(BF16), a2_ref[...]))
    g = _dot(_sigmoid(pre[:, 128:256]).astype(BF16), g2_ref[...])
    if has_vres:
        vf = vf_ref[...].reshape(m, w)
        v = v + (vf - v) * _sigmoid(row(V_V0) + _dot(pre[:, 256:320].astype(BF16), v2_ref[...]))
    else:
        vout_ref[...] = v.reshape(nb, nj * L, w)

    kk = k * row(V_KK)
    k = k * (1.0 + (a - 1.0) * row(V_KA))
    kk_sq, rk_sum = _head_sums([kk * kk, r * k * row(V_RK)], hsum)
    kk = kk * lax.rsqrt(jnp.maximum(kk_sq, 1e-24))
    bonus = rk_sum * v

    pair = 2 * L
    mi = lax.broadcasted_iota(jnp.int32, (pair, pair), 0)
    mj = lax.broadcasted_iota(jnp.int32, (pair, pair), 1)
    same_chunk = (mi // L) == (mj // L)
    tri = jnp.where(jnp.logical_and(same_chunk, mj <= mi), 1.0, 0.0).astype(BF16)
    parts = _split3(logw)
    cl = jnp.concatenate([sum(_dot(tri, part[lo:lo + pair]) for part in parts) for lo in range(0, m, pair)], axis=0)
    e_neg = jnp.exp(-cl)
    a_t = ((-kk) * jnp.exp(cl - logw)).astype(BF16)
    b_t = ((kk * a) * e_neg).astype(BF16)
    k_t = (k * e_neg).astype(BF16)
    r_t = (r * jnp.exp(cl)).astype(BF16)
    v_b = v.astype(BF16)

    pw_lanes = 2 * HEAD_DIM
    lane = lax.broadcasted_iota(jnp.int32, (L, pw_lanes), 1)
    step = lax.broadcasted_iota(jnp.int32, (L, pw_lanes), 0)
    head0 = lane < HEAD_DIM
    upper_strict = step < lane % HEAD_DIM
    lower_incl = lane % HEAD_DIM <= step
    eye = (step == lane % HEAD_DIM).astype(F32)

    def per_head_rows(x):
        zero = jnp.zeros_like(x)
        return jnp.concatenate([jnp.where(head0, x, zero), jnp.where(head0, zero, x)], axis=0)

    chains = []
    for b in range(nb):
        for j in range(nj):
            lo = (b * nj + j) * L
            rows = slice(lo, lo + L)
            v_tt = v[rows].T.astype(BF16)
            p_end = jnp.exp(cl[lo + L - 1:lo + L, :])
            for p in range(N_HEADS // 2):
                sl = slice(p * pw_lanes, (p + 1) * pw_lanes)
                vtt = jnp.concatenate([v_tt[p * pw_lanes:p * pw_lanes + HEAD_DIM, :],
                                       v_tt[p * pw_lanes + HEAD_DIM:(p + 1) * pw_lanes, :]], axis=1)
                chains.append(dict(b=b, j=j, p=p, rows=rows, sl=sl, a_bd=per_head_rows(a_t[rows, sl]),
                                   b_bd=per_head_rows(b_t[rows, sl]), k_bd=per_head_rows(k_t[rows, sl]),
                                   bk=jnp.concatenate([b_t[rows, sl], k_t[rows, sl]], axis=0),
                                   r=r_t[rows, sl], v_bd=per_head_rows(v_b[rows, sl]), vtt=vtt, p_end=p_end[:, sl]))
    for c in chains:
        both = _dot_nt(c["bk"], c["a_bd"])
        c["tab"] = jnp.where(upper_strict, both[:L], 0.0)
        c["tak_bd"] = per_head_rows(jnp.where(upper_strict, both[L:], 0.0).astype(BF16))
        rbk = _dot_nt(c["r"], jnp.concatenate([c["b_bd"], c["k_bd"]], axis=0))
        c["trb"] = jnp.where(lower_incl, rbk[:, :pw_lanes], 0.0).astype(BF16)
        c["trk"] = jnp.where(lower_incl, rbk[:, pw_lanes:], 0.0).astype(BF16)
    t_idx = lane % HEAD_DIM
    coupling = lambda sz: jnp.logical_and(jnp.logical_and(step // (2 * sz) == t_idx // (2 * sz),
                                                         step // sz != t_idx // sz), step < t_idx)
    for c in chains:
        c["inv"] = eye + jnp.where(coupling(1), c["tab"], 0.0)
    sz = 2
    while sz < L:
        mask = coupling(sz)
        for c in chains:
            inv_bd = per_head_rows(c["inv"].astype(BF16))
            off_bd = per_head_rows(jnp.where(mask, c["tab"], 0.0).astype(BF16))
            c["inv"] = c["inv"] + _dot(_dot(c["inv"].astype(BF16), off_bd).astype(BF16), inv_bd)
        sz *= 2
    for c in chains:
        c["inv_bd"] = per_head_rows(c["inv"].astype(BF16))
    state = {(b, p): st_ref[b, p] for b in range(nb) for p in range(N_HEADS // 2)}
    for j in range(nj):
        group = [c for c in chains if c["j"] == j]
        for c in group:
            c["st"] = state[c["b"], c["p"]]
            c["stb"] = c["st"].astype(BF16)
            c["rhs"] = _dot_nt(c["stb"], c["a_bd"]) + _dot(c["vtt"], c["tak_bd"])
        for c in group:
            c["ut"] = _dot(c["rhs"].astype(BF16), c["inv_bd"]).astype(BF16)
        for c in group:
            state[c["b"], c["p"]] = (c["st"] + _dot(c["ut"], c["b_bd"]) + _dot(c["vtt"], c["k_bd"])) * c["p_end"]
        for c in group:
            lhs = jnp.concatenate([c["r"], c["trb"]], axis=1)
            rhs = jnp.concatenate([per_head_rows(c["stb"]), per_head_rows(c["ut"])], axis=1)
            ysc_ref[c["rows"], c["sl"]] = _dot_nt(lhs, rhs) + _dot(c["trk"], c["v_bd"])
    for (b, p), value in state.items():
        st_ref[b, p] = value

    y = ysc_ref[...]
    mean = _head_sums([y], hsum)[0] * (1.0 / HEAD_DIM)
    dev = y - mean
    var = _head_sums([dev * dev], hsum)[0] * (1.0 / HEAD_DIM)
    y = dev * lax.rsqrt(var + RWKV_GN_EPS) * row(V_GNW) + row(V_GNB)
    y_ref[...] = ((y + bonus) * g).reshape(nb, nj * L, w)


def _rwkv(rkv, lora, v_first, vec, w2, a2, g2, v2, hsum):
    b, s, _ = rkv.shape
    n_chunks = s // RWKV_CHUNK
    L = RWKV_CHUNK * max(k for k in (4, 2, 1) if n_chunks % k == 0)
    has_vres = v_first is not None
    row = lambda wd: pl.BlockSpec((b, L, wd), lambda i: (0, i, 0))
    ins = [rkv, lora] + ([v_first] if has_vres else []) + [vec, w2, a2, g2] + ([v2] if has_vres else []) + [hsum]
    in_specs = [row(3 * WIDTH), row(2 * LORA_HALF)] + ([row(WIDTH)] if has_vres else [])
    in_specs += [_const_spec(c.shape) for c in ins[len(in_specs):]]
    y_shape = jax.ShapeDtypeStruct((b, s, WIDTH), F32)
    out = pl.pallas_call(
        functools.partial(_rwkv_body, has_vres),
        grid=(s // L,),
        in_specs=in_specs,
        out_specs=row(WIDTH) if has_vres else [row(WIDTH), row(WIDTH)],
        out_shape=y_shape if has_vres else [y_shape, y_shape],
        scratch_shapes=[pltpu.VMEM((b, N_HEADS // 2, HEAD_DIM, 2 * HEAD_DIM), F32),
                        pltpu.VMEM((b, 1, 3 * WIDTH + LORA_HALF), F32),
                        pltpu.VMEM((b * L, WIDTH), F32)],
        compiler_params=_params(("arbitrary",)),
        name="rwkv_l1" if has_vres else "rwkv_l0",
    )(*ins)
    return (out, v_first) if has_vres else (out[0], out[1])


def _from_residue_view(ref, dil, unperm):
    if dil == 1:
        return ref[...].astype(F32)
    width = ref.shape[1] // dil
    stacked = jnp.concatenate([ref[:, r * width:(r + 1) * width] for r in range(dil)], axis=0)
    if stacked.dtype == BF16:
        return _dot(unperm, stacked)
    hi, lo = _split2(stacked)
    return _dot(unperm, hi) + _dot(unperm, lo)


def _merge_ffn_body(final, x_ref, h_ref, o1_ref, o2_ref, o3_ref, l1_ref, l2_ref, l3_ref, yb_ref, yc_ref, yd_ref,
                    wg_ref, wb_ref, wo_ref, exp_ref, up2_ref, up3_ref,
                    gn_ref, fg_ref, fu_ref, fd_ref, gf_ref, out_ref):
    hb = h_ref[...]
    unperms = (None, up2_ref[...], up3_ref[...])
    o1, o2, o3 = (_from_residue_view(r, d, p) for r, d, p in zip((o1_ref, o2_ref, o3_ref), DILATIONS, unperms))
    l1, l2, l3 = (_from_residue_view(r, d, p) for r, d, p in zip((l1_ref, l2_ref, l3_ref), DILATIONS, unperms))
    m = jnp.maximum(jnp.maximum(l1, l2), l3)
    e1, e2, e3 = jnp.exp(l1 - m), jnp.exp(l2 - m), jnp.exp(l3 - m)
    inv = 1.0 / (e1 + e2 + e3)
    expand = exp_ref[...]
    ya = (_dot_sel(e1 * inv, expand) * o1 + _dot_sel(e2 * inv, expand) * o2 + _dot_sel(e3 * inv, expand) * o3)
    merged = None
    for n, y in enumerate((ya, yb_ref[...], yc_ref[...], yd_ref[...])):
        gate = _sigmoid(_dot(hb, wg_ref[:, n * D_MODEL:(n + 1) * D_MODEL]))
        term = gate * _dot(y.astype(BF16), wb_ref[n])
        merged = term if merged is None else merged + term
    x = x_ref[...] + _dot(merged.astype(BF16), wo_ref[...])
    h2 = _rms(x, gn_ref[...]).astype(BF16)
    gate = _dot(h2, fg_ref[...])
    up = _dot(h2, fu_ref[...])
    y = x + _dot((gate * _sigmoid(gate) * up).astype(BF16), fd_ref[...])
    out_ref[...] = _rms(y, gf_ref[...]) if final else y


def _merge_ffn(x, h, os, ls, yb, yc, yd, wg, wb, wo, expand, gn, fg, fu, fd, gf, final, tm=256):
    b, s, d = x.shape
    row = lambda wd: pl.BlockSpec((None, tm, wd), lambda bi, i: (bi, i, 0))
    res = lambda wd: [pl.BlockSpec((None, tm // dil, dil * wd), lambda bi, i: (bi, i, 0)) for dil in DILATIONS]
    consts = (wg, wb, wo, expand) + tuple(_dilated_perm(tm, dil).T for dil in DILATIONS[1:]) + (gn, fg, fu, fd, gf)
    return pl.pallas_call(
        functools.partial(_merge_ffn_body, final),
        grid=(b, s // tm),
        in_specs=[row(d), row(d)] + res(WIDTH) + res(LANES)
        + [_seg_view_spec(s, tm), row(WIDTH), row(WIDTH)] + [_const_spec(c.shape) for c in consts],
        out_specs=row(d),
        out_shape=jax.ShapeDtypeStruct((b, s, d), F32),
        compiler_params=_params(("parallel", "parallel")),
        name="merge_ffn_final" if final else "merge_ffn",
    )(x, h, *os, *ls, yb, yc, yd, *consts)


def _rope_tables(s):
    inv = 1.0 / (ROPE_THETA ** (jnp.arange(0, HEAD_DIM, 2, dtype=F32) / HEAD_DIM))
    ang = jnp.arange(s, dtype=F32)[:, None] * inv[None, :]
    cos, sin = jnp.cos(ang), jnp.sin(ang)
    reps = LANES // HEAD_DIM
    cos_t = jnp.tile(jnp.concatenate([cos, cos], axis=-1), (1, reps))
    sin_t = jnp.tile(jnp.concatenate([-sin, sin], axis=-1), (1, reps))
    return cos_t, sin_t


def _block_diag(blocks):
    g, r, c = blocks.shape
    eye = jnp.eye(g, dtype=blocks.dtype)
    return (eye[:, None, :, None] * blocks[:, :, None, :]).reshape(g * r, g * c)


def _s5_params(lam_re, lam_im, log_dt, b_re, b_im, c_re, c_im):
    lr, li = lam_re.astype(F32), lam_im.astype(F32)
    dt = jnp.exp(log_dt.astype(F32))[:, None]
    mag = jnp.exp(lr * dt)
    ab_re, ab_im = mag * jnp.cos(li * dt), mag * jnp.sin(li * dt)
    nr = ab_re - 1.0
    den = lr * lr + li * li
    f_re = (nr * lr + ab_im * li) / den
    f_im = (ab_im * lr - nr * li) / den
    bb_re = f_re[..., None] * b_re - f_im[..., None] * b_im
    bb_im = f_re[..., None] * b_im + f_im[..., None] * b_re
    gh = S5_GROUPS // S5_HALVES
    swap = lambda t: jnp.transpose(t, (0, 2, 1))
    halves = lambda t: [t[h * gh:(h + 1) * gh] for h in range(S5_HALVES)]
    bbd = jnp.stack([jnp.concatenate([_block_diag(swap(re)), _block_diag(swap(im))], axis=-1)
                     for re, im in zip(halves(bb_re), halves(bb_im))]).astype(BF16)
    cbd = jnp.stack([jnp.concatenate([_block_diag(swap(re)), -_block_diag(swap(im))], axis=0)
                     for re, im in zip(halves(c_re), halves(c_im))]).astype(BF16)
    flat = lambda t: [h.reshape(1, -1) for h in halves(t)]
    cat = lambda parts: jnp.concatenate(parts, axis=-1)
    a_plain = cat([cat([re, im]) for re, im in zip(flat(ab_re), flat(ab_im))])
    a_same = cat([cat([re, re]) for re in flat(ab_re)])
    a_cross = cat([cat([-im, im]) for im in flat(ab_im)])
    rep = lambda t: jnp.broadcast_to(t, (S5_SEGS, t.shape[-1]))
    return bbd, rep(a_same), rep(a_cross), a_plain, cbd


def _lora_weights(mu_wag, w1, a1, g1, mu_v, v1):
    d = w1.shape[0]
    pad = lambda t: jnp.pad(t, ((0, 0), (0, LORA_HALF - t.shape[1])))
    v1 = jnp.zeros((d, 32), F32) if v1 is None else v1
    mu_v = jnp.zeros((d,), F32) if mu_v is None else mu_v
    mats = (w1, a1, g1, v1)
    mus = (mu_wag[0], mu_wag[1], mu_wag[2], mu_v)
    keep = pad(jnp.concatenate([m * (1.0 - mu)[:, None] for m, mu in zip(mats, mus)], axis=1))
    shifted = pad(jnp.concatenate([m * mu[:, None] for m, mu in zip(mats, mus)], axis=1))
    return jnp.concatenate([keep, shifted], axis=1).astype(BF16)


def kernel(x, norm_mix, w_in, s5_lam_re, s5_lam_im, s5_log_dt, s5_b_re, s5_b_im, s5_c_re, s5_c_im, s5_d, s5_w_glu, s5_b_glu, rw_mu_rkv, rw_mu_wag, rw_w0, rw_w1, rw_w2, rw_a0, rw_a1, rw_a2, rw_g1, rw_g2, rw_k_k, rw_k_a, rw_r_k, rw_gn_w, rw_gn_b, rw_mu_v, rw_v0, rw_v1, rw_v2, lru_conv_w, lru_conv_b, lru_w_a, lru_b_a, lru_w_x, lru_b_x, lru_lam, w_branch, w_out, norm_ffn, w_ffn_gate, w_ffn_up, w_ffn_down, norm_final):
    b, s, d = x.shape
    depth = w_in.shape[0]
    w = WIDTH
    cos_t, sin_t = _rope_tables(s)
    head_id = jnp.arange(w) // HEAD_DIM
    hsum = (head_id[:, None] == head_id[None, :]).astype(BF16)
    expand = (jnp.arange(LANES)[:, None] == head_id[None, :]).astype(BF16)
    bf = lambda t: t.astype(BF16)
    row = lambda t: t.reshape(1, -1).astype(F32)

    v_first = None
    for l in range(depth):
        wl = w_in[l]
        cols = np.cumsum((0, w, w, w, w, w, w, w, w))
        seg = lambda i, j: bf(wl[:, cols[i]:cols[j]])
        has_vres = l > 0
        wlora = _lora_weights(rw_mu_wag[l], rw_w1[l], rw_a1[l], rw_g1[l],
                              rw_mu_v[l - 1] if has_vres else None, rw_v1[l - 1] if has_vres else None)
        lru_consts = (lru_conv_w[l].astype(F32), row(lru_conv_b[l]), bf(_block_diag(lru_w_a[l])), row(lru_b_a[l]),
                      bf(_block_diag(lru_w_x[l])), row(lru_b_x[l]), row(lru_lam[l]))
        h, u, rkv, y_d, lora, *qkv_views = _inproj(
            x, row(norm_mix[l]), cos_t, sin_t, seg(0, 2), seg(2, 3), seg(3, 4), seg(4, 7), seg(7, 8), wlora, lru_consts)

        att = [_attention(view, dil) for view, dil in zip(qkv_views, DILATIONS)]

        s5p = _s5_params(s5_lam_re[l], s5_lam_im[l], s5_log_dt[l], s5_b_re[l], s5_b_im[l], s5_c_re[l], s5_c_im[l])
        y_b = _s5(u, *s5p, row(s5_d[l]), bf(s5_w_glu[l]), row(s5_b_glu[l]))

        zero = jnp.zeros((w,), F32)
        vec = jnp.stack([rw_mu_rkv[l, 0], rw_mu_rkv[l, 1], rw_mu_rkv[l, 2], rw_w0[l], rw_a0[l],
                         rw_v0[l - 1] if has_vres else zero, rw_k_k[l], rw_k_a[l], rw_r_k[l].reshape(-1),
                         rw_gn_w[l], rw_gn_b[l]] + [zero] * 5).astype(F32)
        v2 = bf(jnp.pad(rw_v2[l - 1], ((0, 32), (0, 0)))) if has_vres else None
        y_c, v_c = _rwkv(rkv, lora, v_first, vec, bf(rw_w2[l]), bf(rw_a2[l]), bf(rw_g2[l]), v2, hsum)
        if l == 0:
            v_first = v_c

        x = _merge_ffn(x, h, [o for o, _ in att], [ls for _, ls in att], y_b, y_c, y_d,
                       bf(wl[:, cols[8]:]), bf(w_branch[l]), bf(w_out[l]), expand,
                       row(norm_ffn[l]), bf(w_ffn_gate[l]), bf(w_ffn_up[l]), bf(w_ffn_down[l]), row(norm_final),
                       final=(l == depth - 1))
    return x
```

```python
import functools
import math

import jax
import jax.numpy as jnp
import numpy as np
from jax import lax
from jax.experimental import pallas as pl
from jax.experimental.pallas import tpu as pltpu

F32 = jnp.float32
BF16 = jnp.bfloat16

D_MODEL = 1024
N_HEADS = 8
HEAD_DIM = 64
WIDTH = N_HEADS * HEAD_DIM
DILATIONS = (1, 4, 16)
ATT_BLOCK = 128
ROPE_THETA = 10000.0
S5_GROUPS = 32
S5_GROUP = 16
S5_STATE = 64
S5_NSTATE = S5_GROUPS * S5_STATE
RWKV_GN_EPS = 64e-5
RWKV_CHUNK = 64
LORA_HALF = 384
LRU_BLOCKS = 8
LRU_C = 8.0
D_FF = 2816
RMS_EPS = 1e-6
NEG_BIG = -1e30
Q_SCALE = math.log2(math.e) * HEAD_DIM ** -0.5

LANES = 128
SUBLANES = 8
S5_SEGS = SUBLANES
S5_HALVES = 2
S5_HALF_STATES = S5_NSTATE // S5_HALVES
VMEM_LIMIT = 56 * 1024 * 1024

NT_DIMS = (((1,), (1,)), ((), ()))


def _dot(a, b):
    return jnp.dot(a, b, preferred_element_type=F32)


def _dot_nt(a, b):
    return lax.dot_general(a, b, NT_DIMS, preferred_element_type=F32)


def _split2(x):
    hi = x.astype(BF16)
    lo = (x - hi.astype(F32)).astype(BF16)
    return hi, lo


def _split3(x):
    hi = x.astype(BF16)
    r1 = x - hi.astype(F32)
    mid = r1.astype(BF16)
    lo = (r1 - mid.astype(F32)).astype(BF16)
    return hi, mid, lo


def _dot_sel(x, sel):
    hi, lo = _split2(x)
    return _dot(hi, sel) + _dot(lo, sel)


def _head_sums(xs, hsum):
    m = xs[0].shape[0]
    parts = []
    for x in xs:
        parts += list(_split2(x))
    stacked = jnp.concatenate(parts, axis=0)
    half = hsum.shape[0] // 2
    block = hsum[:half, :half]
    out = jnp.concatenate([_dot(stacked[:, :half], block), _dot(stacked[:, half:], block)], axis=-1)
    return [out[2 * j * m:(2 * j + 1) * m] + out[(2 * j + 1) * m:(2 * j + 2) * m] for j in range(len(xs))]


def _sigmoid(x):
    return 0.5 * jnp.tanh(0.5 * x) + 0.5


def _softplus(z):
    return jnp.maximum(z, 0.0) + jnp.log1p(jnp.exp(-jnp.abs(z)))


def _rms(x, g):
    ms = jnp.mean(x * x, axis=-1, keepdims=True)
    return x * lax.rsqrt(ms + RMS_EPS) * g


def _const_spec(shape):
    n = len(shape)
    return pl.BlockSpec(shape, lambda *_: (0,) * n, pipeline_mode=pl.Buffered(1))


def _params(sem):
    return pltpu.CompilerParams(dimension_semantics=sem, vmem_limit_bytes=VMEM_LIMIT)


def _seg_view_spec(s, tm):
    tiles_per_seg = s // S5_SEGS // tm
    return pl.BlockSpec((None, tm, WIDTH), lambda bi, i: (bi, i % tiles_per_seg, i // tiles_per_seg))


def _dilated_perm(tm, dil):
    dst = np.arange(tm)
    src = (dst % (tm // dil)) * dil + dst // (tm // dil)
    return jnp.asarray(src[:, None] == np.arange(tm)[None, :], BF16)


def _lru_apply(x, cw_ref, cb_ref, wa_ref, ba_ref, wx_ref, bx_ref, lam_ref, o_ref, a_ref, carry_ref, h_ref):
    tb = x.shape[0]
    full = jnp.concatenate([carry_ref[...], x], axis=0)
    carry_ref[...] = x[tb - SUBLANES:]
    cw = cw_ref[...]
    taps = cw.shape[0]
    xc = cb_ref[...] + cw[taps - 1:taps, :] * x
    for j in range(1, taps):
        xc = xc + cw[taps - 1 - j:taps - j, :] * pltpu.roll(full, j, 0)[SUBLANES:]
    xcb = xc.astype(BF16)
    half = xcb.shape[1] // 2
    gate = lambda w_ref: jnp.concatenate([_dot(xcb[:, :half], w_ref[0]), _dot(xcb[:, half:], w_ref[1])], axis=-1)
    r = _sigmoid(gate(wa_ref) + ba_ref[...])
    i = _sigmoid(gate(wx_ref) + bx_ref[...])
    log_a = -LRU_C * r * _softplus(-lam_ref[...])
    a = jnp.exp(log_a)
    bb = jnp.sqrt(-jnp.tanh(log_a) * (a * a + 1.0)) * (i * xc)
    in_tile = lax.broadcasted_iota(jnp.int32, (tb, 1), 0) % SUBLANES
    for k in (1, 2, 4):
        keep = in_tile >= k
        bb = jnp.where(keep, a * pltpu.roll(bb, k, 0) + bb, bb)
        a = jnp.where(keep, a * pltpu.roll(a, k, 0), a)
    a_ref[...] = a
    o_ref[...] = bb
    h = h_ref[0:1, :]
    for n in range(tb // SUBLANES):
        rows = slice(n * SUBLANES, (n + 1) * SUBLANES)
        res = a_ref[rows, :] * h + o_ref[rows, :]
        o_ref[rows, :] = res
        h = res[SUBLANES - 1:SUBLANES, :]
    h_ref[0:1, :] = h


def _inproj_body(x_ref, g_ref, cos_ref, sin_ref, wqk_ref, wv_ref, wu_ref, wrkv_ref, wlru_ref, wlora_ref, *rest):
    n_dil = len(DILATIONS) - 1
    lru_consts, rest = rest[:7], rest[7:]
    perm_refs, (h_ref, u_ref, rkv_ref, lru_ref, lora_ref, qkv_ref), rest = rest[:n_dil], rest[n_dil:n_dil + 6], rest[n_dil + 6:]
    dil_refs, lru_scratch = rest[:n_dil], rest[n_dil:]
    tm = x_ref.shape[0]

    @pl.when(pl.program_id(1) == 0)
    def _():
        for ref in lru_scratch[1:]:
            ref[...] = jnp.zeros_like(ref)

    hb = _rms(x_ref[...], g_ref[...]).astype(BF16)
    h_ref[...] = hb
    _lru_apply(_dot(hb, wlru_ref[...]), *lru_consts, lru_ref, *lru_scratch)
    qk = _dot(hb, wqk_ref[...])
    reps = 2 * WIDTH // LANES
    cos = jnp.concatenate([cos_ref[...]] * reps, axis=-1)
    sin = jnp.concatenate([sin_ref[...]] * reps, axis=-1)
    lane = lax.broadcasted_iota(jnp.int32, qk.shape, 1)
    first_half = (lane % HEAD_DIM) < (HEAD_DIM // 2)
    partner = jnp.where(first_half,
                        pltpu.roll(qk, 2 * WIDTH - HEAD_DIM // 2, 1),
                        pltpu.roll(qk, HEAD_DIM // 2, 1))
    rot = qk * cos + partner * sin
    qkv = jnp.concatenate([(rot[:, :WIDTH] * Q_SCALE).astype(BF16), rot[:, WIDTH:].astype(BF16),
                           _dot(hb, wv_ref[...]).astype(BF16)], axis=-1)
    qkv_ref[...] = qkv
    for dil, perm_ref, out_ref in zip(DILATIONS[1:], perm_refs, dil_refs):
        grouped = _dot(perm_ref[...], qkv).astype(BF16)
        rows = tm // dil
        for r in range(dil):
            out_ref[:, r * 3 * WIDTH:(r + 1) * 3 * WIDTH] = grouped[r * rows:(r + 1) * rows]
    u_ref[...] = _dot(hb, wu_ref[...])
    rkv_ref[...] = _dot(hb, wrkv_ref[...])
    lora_ref[...] = _dot(hb, wlora_ref[...])


def _inproj(x, g, cos, sin, wqk, wv, wu, wrkv, wlru, wlora, lru_consts, tm=256):
    b, s, d = x.shape
    grid = (b, s // tm)
    row = lambda w: pl.BlockSpec((None, tm, w), lambda bi, i: (bi, i, 0))
    tab = pl.BlockSpec((tm, LANES), lambda bi, i: (i, 0))
    widths = (d, WIDTH, 3 * WIDTH, WIDTH, 2 * LORA_HALF, 3 * WIDTH)
    dtypes = (BF16, F32, F32, F32, F32, BF16)
    out_specs = [row(w) for w in widths]
    out_shape = [jax.ShapeDtypeStruct((b, s, w), dt) for w, dt in zip(widths, dtypes)]
    out_specs[1] = _seg_view_spec(s, tm)
    out_shape[1] = jax.ShapeDtypeStruct((b, s // S5_SEGS, S5_SEGS * WIDTH), F32)
    for dil in DILATIONS[1:]:
        out_specs.append(pl.BlockSpec((None, tm // dil, dil * 3 * WIDTH), lambda bi, i: (bi, i, 0)))
        out_shape.append(jax.ShapeDtypeStruct((b, s // dil, dil * 3 * WIDTH), BF16))
    consts = (wqk, wv, wu, wrkv, wlru, wlora) + tuple(lru_consts) + tuple(_dilated_perm(tm, dil) for dil in DILATIONS[1:])
    return pl.pallas_call(
        _inproj_body,
        grid=grid,
        in_specs=[row(d), _const_spec((1, d)), tab, tab] + [_const_spec(c.shape) for c in consts],
        out_specs=out_specs,
        out_shape=out_shape,
        scratch_shapes=[pltpu.VMEM((tm, WIDTH), F32), pltpu.VMEM((SUBLANES, WIDTH), F32), pltpu.VMEM((SUBLANES, WIDTH), F32)],
        compiler_params=_params(("parallel", "arbitrary")),
        name="inproj",
    )(x, g, cos, sin, *consts)


def _attn_body(q_ref, kc_ref, kp_ref, vc_ref, o_ref, l_ref, vt_ref):
    i = pl.program_id(2)
    blk = ATT_BLOCK
    n_sub = q_ref.shape[0] // blk

    @pl.when(i == 0)
    def _():
        vt_ref[...] = jnp.zeros_like(vt_ref)

    rows = [slice(j * blk, (j + 1) * blk) for j in range(n_sub)]
    vts = [vt_ref[...]] + [vc_ref[r, :].astype(F32).T.astype(BF16) for r in rows]
    vt_ref[...] = vts[-1]
    key = lax.broadcasted_iota(jnp.int32, (2 * blk, blk), 0)
    qry = lax.broadcasted_iota(jnp.int32, (2 * blk, blk), 1)
    in_cur = jnp.logical_and(key >= blk, key - blk <= qry)
    in_prev = jnp.logical_and(key < blk, key >= qry)
    masks = [jnp.logical_or(in_cur, jnp.logical_and(in_prev, i > 0))] + [jnp.logical_or(in_cur, in_prev)] * (n_sub - 1)
    heads = [slice(h * HEAD_DIM, (h + 1) * HEAD_DIM) for h in range(N_HEADS)]
    pairs = [(j, sl) for j in range(n_sub) for sl in heads]
    prev_keys = lambda j, sl: kp_ref[:, sl] if j == 0 else kc_ref[rows[j - 1], sl]
    keys = [jnp.concatenate([prev_keys(j, sl), kc_ref[rows[j], sl]], axis=0) for j, sl in pairs]
    st = [jnp.where(masks[j], _dot_nt(k, q_ref[rows[j], sl]), NEG_BIG) for k, (j, sl) in zip(keys, pairs)]
    ms = [jnp.max(s, axis=0, keepdims=True) for s in st]
    ps = [jnp.exp2(s - m).astype(BF16) for s, m in zip(st, ms)]
    ones = jnp.ones((2 * SUBLANES, 2 * blk), BF16)
    outs, dens = [], []
    for n, (j, sl) in enumerate(pairs):
        vt = jnp.concatenate([vts[j][sl, :], vts[j + 1][sl, :]], axis=1)
        res = _dot(jnp.concatenate([vt, ones], axis=0), ps[n])
        dens.append(res[HEAD_DIM:HEAD_DIM + 1, :])
        outs.append(res[:HEAD_DIM, :] * (1.0 / dens[n]))
    pad = [jnp.zeros((LANES - N_HEADS, blk), F32)]
    for j in range(n_sub):
        mine = slice(j * N_HEADS, (j + 1) * N_HEADS)
        o_ref[rows[j], :] = jnp.concatenate(outs[mine], axis=0).T.astype(BF16)
        lse = jnp.concatenate([(m + jnp.log2(d)) * math.log(2.0) for m, d in zip(ms[mine], dens[mine])] + pad, axis=0)
        l_ref[rows[j], :] = lse.T


def _attention(qkv_view, dil):
    b, sub, _ = qkv_view.shape
    w = WIDTH
    n_blocks = sub // ATT_BLOCK
    n_sub = max(k for k in (4, 2, 1) if n_blocks % k == 0)
    step = n_sub * ATT_BLOCK
    cur = lambda j: pl.BlockSpec((None, step, w), lambda bi, r, i: (bi, i, 3 * r + j))
    prev_k = pl.BlockSpec((None, ATT_BLOCK, w), lambda bi, r, i: (bi, jnp.maximum(i * n_sub - 1, 0), 3 * r + 1))
    return pl.pallas_call(
        _attn_body,
        grid=(b, dil, n_blocks // n_sub),
        in_specs=[cur(0), cur(1), prev_k, cur(2)],
        out_specs=[pl.BlockSpec((None, step, w), lambda bi, r, i: (bi, i, r)),
                   pl.BlockSpec((None, step, LANES), lambda bi, r, i: (bi, i, r))],
        out_shape=[jax.ShapeDtypeStruct((b, sub, dil * w), BF16),
                   jax.ShapeDtypeStruct((b, sub, dil * LANES), F32)],
        scratch_shapes=[pltpu.VMEM((w, ATT_BLOCK), BF16)],
        compiler_params=_params(("parallel", "parallel", "arbitrary")),
        name=f"attn_d{dil}",
    )(*([qkv_view] * 4))


def _gelu_tanh(x):
    return 0.5 * x * (1.0 + jnp.tanh(math.sqrt(2.0 / math.pi) * (x + 0.044715 * (x * x * x))))


def _s5_slab(u_ref, perm_ref):
    u_stack = jnp.concatenate([u_ref[:, sg * WIDTH:(sg + 1) * WIDTH] for sg in range(S5_SEGS)], axis=0)
    u_slab = _dot(perm_ref[...], u_stack.astype(BF16)).astype(BF16)
    return u_stack, u_slab


def _s5_scan(u_slab, bbd_ref, asame_ref, across_ref, bu_ref, st_ref, store):
    hs = S5_HALF_STATES
    cin = WIDTH // S5_HALVES
    cols = [slice(half * 2 * hs, (half + 1) * 2 * hs) for half in range(S5_HALVES)]
    for half in range(S5_HALVES):
        bu_ref[half] = _dot(u_slab[:, half * cin:(half + 1) * cin], bbd_ref[half])
    xs = [st_ref[:, c] for c in cols]
    for t in range(bu_ref.shape[1] // S5_SEGS):
        rows = slice(t * S5_SEGS, (t + 1) * S5_SEGS)
        for half, c in enumerate(cols):
            x = xs[half]
            swapped = jnp.concatenate([x[:, hs:], x[:, :hs]], axis=-1)
            x = asame_ref[:, c] * x + across_ref[:, c] * swapped + bu_ref[half, rows, :]
            if store:
                bu_ref[half, rows, :] = x
            xs[half] = x
    for half, c in enumerate(cols):
        st_ref[:, c] = xs[half]


def _s5_state_body(u_ref, perm_ref, bbd_ref, asame_ref, across_ref, end_ref, bu_ref, st_ref):
    @pl.when(pl.program_id(1) == 0)
    def _():
        st_ref[...] = jnp.zeros_like(st_ref)

    _, u_slab = _s5_slab(u_ref, perm_ref)
    _s5_scan(u_slab, bbd_ref, asame_ref, across_ref, bu_ref, st_ref, store=False)

    @pl.when(pl.program_id(1) == pl.num_programs(1) - 1)
    def _():
        end_ref[...] = st_ref[...]


def _cmul(z, w):
    hs = z.shape[-1] // 2
    zr, zi, wr, wi = z[:, :hs], z[:, hs:], w[:, :hs], w[:, hs:]
    return jnp.concatenate([zr * wr - zi * wi, zr * wi + zi * wr], axis=-1)


def _s5_out_body(seg_len, u_ref, end_ref, perm_ref, permt_ref, bbd_ref, asame_ref, across_ref, aplain_ref,
                 cbd_ref, d_ref, wglu_ref, bglu_ref, o_ref, bu_ref, st_ref):
    hs = S5_HALF_STATES
    tt = u_ref.shape[0]

    @pl.when(pl.program_id(1) == 0)
    def _():
        for half in range(S5_HALVES):
            cols = slice(half * 2 * hs, (half + 1) * 2 * hs)
            base = aplain_ref[:, cols]
            power = None
            e = seg_len
            while e:
                if e & 1:
                    power = base if power is None else _cmul(power, base)
                e >>= 1
                if e:
                    base = _cmul(base, base)
            x = jnp.zeros((1, 2 * hs), F32)
            for sg in range(S5_SEGS):
                st_ref[sg:sg + 1, cols] = x
                x = _cmul(x, power) + end_ref[sg:sg + 1, cols]

    u_stack, u_slab = _s5_slab(u_ref, perm_ref)
    _s5_scan(u_slab, bbd_ref, asame_ref, across_ref, bu_ref, st_ref, store=True)
    ys = [_dot(bu_ref[half].astype(BF16), cbd_ref[half]) for half in range(S5_HALVES)]
    hi, lo = _split2(jnp.concatenate(ys, axis=-1))
    y = _dot(permt_ref[...], hi) + _dot(permt_ref[...], lo) + d_ref[...] * u_stack
    z = _dot(_gelu_tanh(y).astype(BF16), wglu_ref[...]) + bglu_ref[...]
    out = z[:, :WIDTH] * _sigmoid(z[:, WIDTH:])
    for sg in range(S5_SEGS):
        o_ref[:, sg * WIDTH:(sg + 1) * WIDTH] = out[sg * tt:(sg + 1) * tt]


def _s5(u_view, bbd, a_same, a_cross, a_plain, cbd, d_skip, wglu, bglu, tt=64):
    b, seg_len, wide = u_view.shape
    m = tt * S5_SEGS
    dst = np.arange(m)
    src = (dst % S5_SEGS) * tt + dst // S5_SEGS
    perm = jnp.asarray(src[:, None] == np.arange(m)[None, :], BF16)
    row = pl.BlockSpec((None, tt, wide), lambda bi, i: (bi, i, 0))
    ends = pl.BlockSpec((None, S5_SEGS, 2 * S5_NSTATE), lambda bi, i: (bi, 0, 0))
    scratch = [pltpu.VMEM((S5_HALVES, m, 2 * S5_HALF_STATES), F32), pltpu.VMEM((S5_SEGS, 2 * S5_NSTATE), F32)]
    consts1 = (perm, bbd, a_same, a_cross)
    seg_end = pl.pallas_call(
        _s5_state_body,
        grid=(b, seg_len // tt),
        in_specs=[row] + [_const_spec(c.shape) for c in consts1],
        out_specs=ends,
        out_shape=jax.ShapeDtypeStruct((b, S5_SEGS, 2 * S5_NSTATE), F32),
        scratch_shapes=scratch,
        compiler_params=_params(("parallel", "arbitrary")),
        name="s5_state",
    )(u_view, *consts1)
    consts2 = (perm, perm.T, bbd, a_same, a_cross, a_plain, cbd, d_skip, wglu, bglu)
    return pl.pallas_call(
        functools.partial(_s5_out_body, seg_len),
        grid=(b, seg_len // tt),
        in_specs=[row, ends] + [_const_spec(c.shape) for c in consts2],
        out_specs=row,
        out_shape=jax.ShapeDtypeStruct((b, seg_len, wide), F32),
        scratch_shapes=scratch,
        compiler_params=_params(("parallel", "arbitrary")),
        name="s5_out",
    )(u_view, seg_end, *consts2)


V_MU_R, V_MU_K, V_MU_V, V_W0, V_A0, V_V0, V_KK, V_KA, V_RK, V_GNW, V_GNB = range(11)


def _rwkv_body(has_vres, *refs):
    if has_vres:
        (rkv_ref, lora_ref, vf_ref, vec_ref, w2_ref, a2_ref, g2_ref, v2_ref, hsum_ref,
         y_ref, st_ref, carry_ref, ysc_ref) = refs
    else:
        (rkv_ref, lora_ref, vec_ref, w2_ref, a2_ref, g2_ref, hsum_ref,
         y_ref, vout_ref, st_ref, carry_ref, ysc_ref) = refs
    L = RWKV_CHUNK
    nb, nj = rkv_ref.shape[0], rkv_ref.shape[1] // L
    m = nb * nj * L
    w = WIDTH

    @pl.when(pl.program_id(0) == 0)
    def _():
        st_ref[...] = jnp.zeros_like(st_ref)
        carry_ref[...] = jnp.zeros_like(carry_ref)

    vec = vec_ref[...]
    row = lambda j: vec[j:j + 1, :]
    hsum = hsum_ref[...]

    rkv = rkv_ref[...].reshape(m, 3 * w)
    lora = lora_ref[...].reshape(m, 2 * LORA_HALF)
    cat = jnp.concatenate([rkv, lora[:, LORA_HALF:]], axis=-1)
    row_id = lax.broadcasted_iota(jnp.int32, (m, 1), 0)
    prev = pltpu.roll(cat, 1, 0)
    for b in range(nb):
        prev = jnp.where(row_id == b * nj * L, carry_ref[b], prev)
        carry_ref[b] = cat[(b + 1) * nj * L - 1:(b + 1) * nj * L, :]
    lerp = lambda j, mu: rkv[:, j * w:(j + 1) * w] + (prev[:, j * w:(j + 1) * w] - rkv[:, j * w:(j + 1) * w]) * mu
    r = lerp(0, row(V_MU_R))
    k = lerp(1, row(V_MU_K))
    v = lerp(2, row(V_MU_V))
    pre = lora[:, :LORA_HALF] + prev[:, 3 * w:]

    wx = row(V_W0) + _dot(jnp.tanh(pre[:, 0:64]).astype(BF16), w2_ref[...])
    logw = -jnp.exp(-_softplus(-wx) - 0.5)
    a = _sigmoid(row(V_A0) + _dot(pre[:, 64:128].astype(BF16), a2_ref[...]))
    g = _dot(_sigmoid(pre[:, 128:256]).astype(BF16), g2_ref[...])
    if has_vres:
        vf = vf_ref[...].reshape(m, w)
        v = v + (vf - v) * _sigmoid(row(V_V0) + _dot(pre[:, 256:320].astype(BF16), v2_ref[...]))
    else:
        vout_ref[...] = v.reshape(nb, nj * L, w)

    kk = k * row(V_KK)
    k = k * (1.0 + (a - 1.0) * row(V_KA))
    kk_sq, rk_sum = _head_sums([kk * kk, r * k * row(V_RK)], hsum)
    kk = kk * lax.rsqrt(jnp.maximum(kk_sq, 1e-24))
    bonus = rk_sum * v

    pair = 2 * L
    mi = lax.broadcasted_iota(jnp.int32, (pair, pair), 0)
    mj = lax.broadcasted_iota(jnp.int32, (pair, pair), 1)
    same_chunk = (mi // L) == (mj // L)
    tri = jnp.where(jnp.logical_and(same_chunk, mj <= mi), 1.0, 0.0).astype(BF16)
    parts = _split3(logw)
    cl = jnp.concatenate([sum(_dot(tri, part[lo:lo + pair]) for part in parts) for lo in range(0, m, pair)], axis=0)
    e_neg = jnp.exp(-cl)
    a_t = ((-kk) * jnp.exp(cl - logw)).astype(BF16)
    b_t = ((kk * a) * e_neg).astype(BF16)
    k_t = (k * e_neg).astype(BF16)
    r_t = (r * jnp.exp(cl)).astype(BF16)
    v_b = v.astype(BF16)

    pw_lanes = 2 * HEAD_DIM
    lane = lax.broadcasted_iota(jnp.int32, (L, pw_lanes), 1)
    step = lax.broadcasted_iota(jnp.int32, (L, pw_lanes), 0)
    head0 = lane < HEAD_DIM
    upper_strict = step < lane % HEAD_DIM
    lower_incl = lane % HEAD_DIM <= step
    eye = (step == lane % HEAD_DIM).astype(F32)

    def per_head_rows(x):
        zero = jnp.zeros_like(x)
        return jnp.concatenate([jnp.where(head0, x, zero), jnp.where(head0, zero, x)], axis=0)

    chains = []
    for b in range(nb):
        for j in range(nj):
            lo = (b * nj + j) * L
            rows = slice(lo, lo + L)
            v_tt = v[rows].T.astype(BF16)
            p_end = jnp.exp(cl[lo + L - 1:lo + L, :])
            for p in range(N_HEADS // 2):
                sl = slice(p * pw_lanes, (p + 1) * pw_lanes)
                vtt = jnp.concatenate([v_tt[p * pw_lanes:p * pw_lanes + HEAD_DIM, :],
                                       v_tt[p * pw_lanes + HEAD_DIM:(p + 1) * pw_lanes, :]], axis=1)
                chains.append(dict(b=b, j=j, p=p, rows=rows, sl=sl, a_bd=per_head_rows(a_t[rows, sl]),
                                   b_bd=per_head_rows(b_t[rows, sl]), k_bd=per_head_rows(k_t[rows, sl]),
                                   bk=jnp.concatenate([b_t[rows, sl], k_t[rows, sl]], axis=0),
                                   r=r_t[rows, sl], v_bd=per_head_rows(v_b[rows, sl]), vtt=vtt, p_end=p_end[:, sl]))
    for c in chains:
        both = _dot_nt(c["bk"], c["a_bd"])
        c["tab"] = jnp.where(upper_strict, both[:L], 0.0)
        c["tak_bd"] = per_head_rows(jnp.where(upper_strict, both[L:], 0.0).astype(BF16))
        rbk = _dot_nt(c["r"], jnp.concatenate([c["b_bd"], c["k_bd"]], axis=0))
        c["trb"] = jnp.where(lower_incl, rbk[:, :pw_lanes], 0.0).astype(BF16)
        c["trk"] = jnp.where(lower_incl, rbk[:, pw_lanes:], 0.0).astype(BF16)
    t_idx = lane % HEAD_DIM
    coupling = lambda sz: jnp.logical_and(jnp.logical_and(step // (2 * sz) == t_idx // (2 * sz),
                                                         step // sz != t_idx // sz), step < t_idx)
    for c in chains:
        c["inv"] = eye + jnp.where(coupling(1), c["tab"], 0.0)
    sz = 2
    while sz < L:
        mask = coupling(sz)
        for c in chains:
            inv_bd = per_head_rows(c["inv"].astype(BF16))
            off_bd = per_head_rows(jnp.where(mask, c["tab"], 0.0).astype(BF16))
            c["inv"] = c["inv"] + _dot(_dot(c["inv"].astype(BF16), off_bd).astype(BF16), inv_bd)
        sz *= 2
    for c in chains:
        c["inv_bd"] = per_head_rows(c["inv"].astype(BF16))
    state = {(b, p): st_ref[b, p] for b in range(nb) for p in range(N_HEADS // 2)}
    for j in range(nj):
        group = [c for c in chains if c["j"] == j]
        for c in group:
            c["st"] = state[c["b"], c["p"]]
            c["stb"] = c["st"].astype(BF16)
            c["rhs"] = _dot_nt(c["stb"], c["a_bd"]) + _dot(c["vtt"], c["tak_bd"])
        for c in group:
            c["ut"] = _dot(c["rhs"].astype(BF16), c["inv_bd"]).astype(BF16)
        for c in group:
            state[c["b"], c["p"]] = (c["st"] + _dot(c["ut"], c["b_bd"]) + _dot(c["vtt"], c["k_bd"])) * c["p_end"]
        for c in group:
            lhs = jnp.concatenate([c["r"], c["trb"]], axis=1)
            rhs = jnp.concatenate([per_head_rows(c["stb"]), per_head_rows(c["ut"])], axis=1)
            ysc_ref[c["rows"], c["sl"]] = _dot_nt(lhs, rhs) + _dot(c["trk"], c["v_bd"])
    for (b, p), value in state.items():
        st_ref[b, p] = value

    y = ysc_ref[...]
    mean = _head_sums([y], hsum)[0] * (1.0 / HEAD_DIM)
    dev = y - mean
    var = _head_sums([dev * dev], hsum)[0] * (1.0 / HEAD_DIM)
    y = dev * lax.rsqrt(var + RWKV_GN_EPS) * row(V_GNW) + row(V_GNB)
    y_ref[...] = ((y + bonus) * g).reshape(nb, nj * L, w)


def _rwkv(rkv, lora, v_first, vec, w2, a2, g2, v2, hsum):
    b, s, _ = rkv.shape
    n_chunks = s // RWKV_CHUNK
    L = RWKV_CHUNK * max(k for k in (4, 2, 1) if n_chunks % k == 0)
    has_vres = v_first is not None
    row = lambda wd: pl.BlockSpec((b, L, wd), lambda i: (0, i, 0))
    ins = [rkv, lora] + ([v_first] if has_vres else []) + [vec, w2, a2, g2] + ([v2] if has_vres else []) + [hsum]
    in_specs = [row(3 * WIDTH), row(2 * LORA_HALF)] + ([row(WIDTH)] if has_vres else [])
    in_specs += [_const_spec(c.shape) for c in ins[len(in_specs):]]
    y_shape = jax.ShapeDtypeStruct((b, s, WIDTH), F32)
    out = pl.pallas_call(
        functools.partial(_rwkv_body, has_vres),
        grid=(s // L,),
        in_specs=in_specs,
        out_specs=row(WIDTH) if has_vres else [row(WIDTH), row(WIDTH)],
        out_shape=y_shape if has_vres else [y_shape, y_shape],
        scratch_shapes=[pltpu.VMEM((b, N_HEADS // 2, HEAD_DIM, 2 * HEAD_DIM), F32),
                        pltpu.VMEM((b, 1, 3 * WIDTH + LORA_HALF), F32),
                        pltpu.VMEM((b * L, WIDTH), F32)],
        compiler_params=_params(("arbitrary",)),
        name="rwkv_l1" if has_vres else "rwkv_l0",
    )(*ins)
    return (out, v_first) if has_vres else (out[0], out[1])


def _from_residue_view(ref, dil, unperm):
    if dil == 1:
        return ref[...].astype(F32)
    width = ref.shape[1] // dil
    stacked = jnp.concatenate([ref[:, r * width:(r + 1) * width] for r in range(dil)], axis=0)
    if stacked.dtype == BF16:
        return _dot(unperm, stacked)
    hi, lo = _split2(stacked)
    return _dot(unperm, hi) + _dot(unperm, lo)


def _merge_ffn_body(final, x_ref, h_ref, o1_ref, o2_ref, o3_ref, l1_ref, l2_ref, l3_ref, yb_ref, yc_ref, yd_ref,
                    wg_ref, wb_ref, wo_ref, exp_ref, up2_ref, up3_ref,
                    gn_ref, fg_ref, fu_ref, fd_ref, gf_ref, out_ref):
    hb = h_ref[...]
    unperms = (None, up2_ref[...], up3_ref[...])
    o1, o2, o3 = (_from_residue_view(r, d, p) for r, d, p in zip((o1_ref, o2_ref, o3_ref), DILATIONS, unperms))
    l1, l2, l3 = (_from_residue_view(r, d, p) for r, d, p in zip((l1_ref, l2_ref, l3_ref), DILATIONS, unperms))
    m = jnp.maximum(jnp.maximum(l1, l2), l3)
    e1, e2, e3 = jnp.exp(l1 - m), jnp.exp(l2 - m), jnp.exp(l3 - m)
    inv = 1.0 / (e1 + e2 + e3)
    expand = exp_ref[...]
    ya = (_dot_sel(e1 * inv, expand) * o1 + _dot_sel(e2 * inv, expand) * o2 + _dot_sel(e3 * inv, expand) * o3)
    merged = None
    for n, y in enumerate((ya, yb_ref[...], yc_ref[...], yd_ref[...])):
        gate = _sigmoid(_dot(hb, wg_ref[:, n * D_MODEL:(n + 1) * D_MODEL]))
        term = gate * _dot(y.astype(BF16), wb_ref[n])
        merged = term if merged is None else merged + term
    x = x_ref[...] + _dot(merged.astype(BF16), wo_ref[...])
    h2 = _rms(x, gn_ref[...]).astype(BF16)
    gate = _dot(h2, fg_ref[...])
    up = _dot(h2, fu_ref[...])
    y = x + _dot((gate * _sigmoid(gate) * up).astype(BF16), fd_ref[...])
    out_ref[...] = _rms(y, gf_ref[...]) if final else y


def _merge_ffn(x, h, os, ls, yb, yc, yd, wg, wb, wo, expand, gn, fg, fu, fd, gf, final, tm=256):
    b, s, d = x.shape
    row = lambda wd: pl.BlockSpec((None, tm, wd), lambda bi, i: (bi, i, 0))
    res = lambda wd: [pl.BlockSpec((None, tm // dil, dil * wd), lambda bi, i: (bi, i, 0)) for dil in DILATIONS]
    consts = (wg, wb, wo, expand) + tuple(_dilated_perm(tm, dil).T for dil in DILATIONS[1:]) + (gn, fg, fu, fd, gf)
    return pl.pallas_call(
        functools.partial(_merge_ffn_body, final),
        grid=(b, s // tm),
        in_specs=[row(d), row(d)] + res(WIDTH) + res(LANES)
        + [_seg_view_spec(s, tm), row(WIDTH), row(WIDTH)] + [_const_spec(c.shape) for c in consts],
        out_specs=row(d),
        out_shape=jax.ShapeDtypeStruct((b, s, d), F32),
        compiler_params=_params(("parallel", "parallel")),
        name="merge_ffn_final" if final else "merge_ffn",
    )(x, h, *os, *ls, yb, yc, yd, *consts)


def _rope_tables(s):
    inv = 1.0 / (ROPE_THETA ** (jnp.arange(0, HEAD_DIM, 2, dtype=F32) / HEAD_DIM))
    ang = jnp.arange(s, dtype=F32)[:, None] * inv[None, :]
    cos, sin = jnp.cos(ang), jnp.sin(ang)
    reps = LANES // HEAD_DIM
    cos_t = jnp.tile(jnp.concatenate([cos, cos], axis=-1), (1, reps))
    sin_t = jnp.tile(jnp.concatenate([-sin, sin], axis=-1), (1, reps))
    return cos_t, sin_t


def _block_diag(blocks):
    g, r, c = blocks.shape
    eye = jnp.eye(g, dtype=blocks.dtype)
    return (eye[:, None, :, None] * blocks[:, :, None, :]).reshape(g * r, g * c)


def _s5_params(lam_re, lam_im, log_dt, b_re, b_im, c_re, c_im):
    lr, li = lam_re.astype(F32), lam_im.astype(F32)
    dt = jnp.exp(log_dt.astype(F32))[:, None]
    mag = jnp.exp(lr * dt)
    ab_re, ab_im = mag * jnp.cos(li * dt), mag * jnp.sin(li * dt)
    nr = ab_re - 1.0
    den = lr * lr + li * li
    f_re = (nr * lr + ab_im * li) / den
    f_im = (ab_im * lr - nr * li) / den
    bb_re = f_re[..., None] * b_re - f_im[..., None] * b_im
    bb_im = f_re[..., None] * b_im + f_im[..., None] * b_re
    gh = S5_GROUPS // S5_HALVES
    swap = lambda t: jnp.transpose(t, (0, 2, 1))
    halves = lambda t: [t[h * gh:(h + 1) * gh] for h in range(S5_HALVES)]
    bbd = jnp.stack([jnp.concatenate([_block_diag(swap(re)), _block_diag(swap(im))], axis=-1)
                     for re, im in zip(halves(bb_re), halves(bb_im))]).astype(BF16)
    cbd = jnp.stack([jnp.concatenate([_block_diag(swap(re)), -_block_diag(swap(im))], axis=0)
                     for re, im in zip(halves(c_re), halves(c_im))]).astype(BF16)
    flat = lambda t: [h.reshape(1, -1) for h in halves(t)]
    cat = lambda parts: jnp.concatenate(parts, axis=-1)
    a_plain = cat([cat([re, im]) for re, im in zip(flat(ab_re), flat(ab_im))])
    a_same = cat([cat([re, re]) for re in flat(ab_re)])
    a_cross = cat([cat([-im, im]) for im in flat(ab_im)])
    rep = lambda t: jnp.broadcast_to(t, (S5_SEGS, t.shape[-1]))
    return bbd, rep(a_same), rep(a_cross), a_plain, cbd


def _lora_weights(mu_wag, w1, a1, g1, mu_v, v1):
    d = w1.shape[0]
    pad = lambda t: jnp.pad(t, ((0, 0), (0, LORA_HALF - t.shape[1])))
    v1 = jnp.zeros((d, 32), F32) if v1 is None else v1
    mu_v = jnp.zeros((d,), F32) if mu_v is None else mu_v
    mats = (w1, a1, g1, v1)
    mus = (mu_wag[0], mu_wag[1], mu_wag[2], mu_v)
    keep = pad(jnp.concatenate([m * (1.0 - mu)[:, None] for m, mu in zip(mats, mus)], axis=1))
    shifted = pad(jnp.concatenate([m * mu[:, None] for m, mu in zip(mats, mus)], axis=1))
    return jnp.concatenate([keep, shifted], axis=1).astype(BF16)


def kernel(x, norm_mix, w_in, s5_lam_re, s5_lam_im, s5_log_dt, s5_b_re, s5_b_im, s5_c_re, s5_c_im, s5_d, s5_w_glu, s5_b_glu, rw_mu_rkv, rw_mu_wag, rw_w0, rw_w1, rw_w2, rw_a0, rw_a1, rw_a2, rw_g1, rw_g2, rw_k_k, rw_k_a, rw_r_k, rw_gn_w, rw_gn_b, rw_mu_v, rw_v0, rw_v1, rw_v2, lru_conv_w, lru_conv_b, lru_w_a, lru_b_a, lru_w_x, lru_b_x, lru_lam, w_branch, w_out, norm_ffn, w_ffn_gate, w_ffn_up, w_ffn_down, norm_final):
    b, s, d = x.shape
    depth = w_in.shape[0]
    w = WIDTH
    cos_t, sin_t = _rope_tables(s)
    head_id = np.arange(w) // HEAD_DIM
    hsum = jnp.asarray(head_id[:, None] == head_id[None, :], BF16)
    expand = jnp.asarray(np.arange(LANES)[:, None] == head_id[None, :], BF16)
    bf = lambda t: t.astype(BF16)
    row = lambda t: t.reshape(1, -1).astype(F32)

    v_first = None
    for l in range(depth):
        wl = w_in[l]
        cols = np.cumsum((0, w, w, w, w, w, w, w, w))
        seg = lambda i, j: bf(wl[:, cols[i]:cols[j]])
        has_vres = l > 0
        wlora = _lora_weights(rw_mu_wag[l], rw_w1[l], rw_a1[l], rw_g1[l],
                              rw_mu_v[l - 1] if has_vres else None, rw_v1[l - 1] if has_vres else None)
        gate_w = lambda t: bf(jnp.stack([_block_diag(t[:LRU_BLOCKS // 2]), _block_diag(t[LRU_BLOCKS // 2:])]))
        lru_consts = (lru_conv_w[l].astype(F32), row(lru_conv_b[l]), gate_w(lru_w_a[l]), row(lru_b_a[l]),
                      gate_w(lru_w_x[l]), row(lru_b_x[l]), row(lru_lam[l]))
        h, u, rkv, y_d, lora, *qkv_views = _inproj(
            x, row(norm_mix[l]), cos_t, sin_t, seg(0, 2), seg(2, 3), seg(3, 4), seg(4, 7), seg(7, 8), wlora, lru_consts)

        att = [_attention(view, dil) for view, dil in zip(qkv_views, DILATIONS)]

        s5p = _s5_params(s5_lam_re[l], s5_lam_im[l], s5_log_dt[l], s5_b_re[l], s5_b_im[l], s5_c_re[l], s5_c_im[l])
        y_b = _s5(u, *s5p, row(s5_d[l]), bf(s5_w_glu[l]), row(s5_b_glu[l]))

        zero = jnp.zeros((w,), F32)
        vec = jnp.stack([rw_mu_rkv[l, 0], rw_mu_rkv[l, 1], rw_mu_rkv[l, 2], rw_w0[l], rw_a0[l],
                         rw_v0[l - 1] if has_vres else zero, rw_k_k[l], rw_k_a[l], rw_r_k[l].reshape(-1),
                         rw_gn_w[l], rw_gn_b[l]] + [zero] * 5).astype(F32)
        v2 = bf(jnp.pad(rw_v2[l - 1], ((0, 32), (0, 0)))) if has_vres else None
        y_c, v_c = _rwkv(rkv, lora, v_first, vec, bf(rw_w2[l]), bf(rw_a2[l]), bf(rw_g2[l]), v2, hsum)
        if l == 0:
            v_first = v_c

        x = _merge_ffn(x, h, [o for o, _ in att], [ls for _, ls in att], y_b, y_c, y_d,
                       bf(wl[:, cols[8]:]), bf(w_branch[l]), bf(w_out[l]), expand,
                       row(norm_ffn[l]), bf(w_ffn_gate[l]), bf(w_ffn_up[l]), bf(w_ffn_down[l]), row(norm_final),
                       final=(l == depth - 1))
    return x
```

```python
import functools
import math

import jax
import jax.numpy as jnp
import numpy as np
from jax import lax
from jax.experimental import pallas as pl
from jax.experimental.pallas import tpu as pltpu

F32 = jnp.float32
BF16 = jnp.bfloat16

D_MODEL = 1024
N_HEADS = 8
HEAD_DIM = 64
WIDTH = N_HEADS * HEAD_DIM
DILATIONS = (1, 4, 16)
ATT_BLOCK = 128
ROPE_THETA = 10000.0
S5_GROUPS = 32
S5_GROUP = 16
S5_STATE = 64
S5_NSTATE = S5_GROUPS * S5_STATE
RWKV_GN_EPS = 64e-5
RWKV_CHUNK = 64
LORA_HALF = 384
LRU_BLOCKS = 8
LRU_C = 8.0
D_FF = 2816
RMS_EPS = 1e-6
NEG_BIG = -1e30
Q_SCALE = math.log2(math.e) * HEAD_DIM ** -0.5

LANES = 128
SUBLANES = 8
S5_SEGS = SUBLANES
S5_HALVES = 2
S5_HALF_STATES = S5_NSTATE // S5_HALVES
VMEM_LIMIT = 56 * 1024 * 1024

NT_DIMS = (((1,), (1,)), ((), ()))


def _dot(a, b):
    return jnp.dot(a, b, preferred_element_type=F32)


def _dot_nt(a, b):
    return lax.dot_general(a, b, NT_DIMS, preferred_element_type=F32)


def _split2(x):
    hi = x.astype(BF16)
    lo = (x - hi.astype(F32)).astype(BF16)
    return hi, lo


def _split3(x):
    hi = x.astype(BF16)
    r1 = x - hi.astype(F32)
    mid = r1.astype(BF16)
    lo = (r1 - mid.astype(F32)).astype(BF16)
    return hi, mid, lo


def _dot_sel(x, sel):
    hi, lo = _split2(x)
    return _dot(hi, sel) + _dot(lo, sel)


def _head_sums(xs, hsum):
    m = xs[0].shape[0]
    parts = []
    for x in xs:
        parts += list(_split2(x))
    stacked = jnp.concatenate(parts, axis=0)
    half = hsum.shape[0] // 2
    block = hsum[:half, :half]
    out = jnp.concatenate([_dot(stacked[:, :half], block), _dot(stacked[:, half:], block)], axis=-1)
    return [out[2 * j * m:(2 * j + 1) * m] + out[(2 * j + 1) * m:(2 * j + 2) * m] for j in range(len(xs))]


def _sigmoid(x):
    return 0.5 * jnp.tanh(0.5 * x) + 0.5


def _softplus(z):
    return jnp.maximum(z, 0.0) + jnp.log1p(jnp.exp(-jnp.abs(z)))


def _rms(x, g):
    ms = jnp.mean(x * x, axis=-1, keepdims=True)
    return x * lax.rsqrt(ms + RMS_EPS) * g


def _const_spec(shape):
    n = len(shape)
    return pl.BlockSpec(shape, lambda *_: (0,) * n, pipeline_mode=pl.Buffered(1))


def _params(sem):
    return pltpu.CompilerParams(dimension_semantics=sem, vmem_limit_bytes=VMEM_LIMIT)


def _seg_view_spec(s, tm):
    tiles_per_seg = s // S5_SEGS // tm
    return pl.BlockSpec((None, tm, WIDTH), lambda bi, i: (bi, i % tiles_per_seg, i // tiles_per_seg))


def _dilated_perm(tm, dil):
    dst = np.arange(tm)
    src = (dst % (tm // dil)) * dil + dst // (tm // dil)
    return jnp.asarray(src[:, None] == np.arange(tm)[None, :], BF16)


def _lru_apply(x, cw_ref, cb_ref, wa_ref, ba_ref, wx_ref, bx_ref, lam_ref, o_ref, a_ref, carry_ref, h_ref):
    tb = x.shape[0]
    full = jnp.concatenate([carry_ref[...], x], axis=0)
    carry_ref[...] = x[tb - SUBLANES:]
    cw = cw_ref[...]
    taps = cw.shape[0]
    xc = cb_ref[...] + cw[taps - 1:taps, :] * x
    for j in range(1, taps):
        xc = xc + cw[taps - 1 - j:taps - j, :] * pltpu.roll(full, j, 0)[SUBLANES:]
    xcb = xc.astype(BF16)
    half = xcb.shape[1] // 2
    gate = lambda w_ref: jnp.concatenate([_dot(xcb[:, :half], w_ref[0]), _dot(xcb[:, half:], w_ref[1])], axis=-1)
    r = _sigmoid(gate(wa_ref) + ba_ref[...])
    i = _sigmoid(gate(wx_ref) + bx_ref[...])
    log_a = -LRU_C * r * _softplus(-lam_ref[...])
    a = jnp.exp(log_a)
    bb = jnp.sqrt(-jnp.tanh(log_a) * (a * a + 1.0)) * (i * xc)
    in_tile = lax.broadcasted_iota(jnp.int32, (tb, 1), 0) % SUBLANES
    for k in (1, 2, 4):
        keep = in_tile >= k
        bb = jnp.where(keep, a * pltpu.roll(bb, k, 0) + bb, bb)
        a = jnp.where(keep, a * pltpu.roll(a, k, 0), a)
    a_ref[...] = a
    o_ref[...] = bb
    h = h_ref[0:1, :]
    for n in range(tb // SUBLANES):
        rows = slice(n * SUBLANES, (n + 1) * SUBLANES)
        res = a_ref[rows, :] * h + o_ref[rows, :]
        o_ref[rows, :] = res
        h = res[SUBLANES - 1:SUBLANES, :]
    h_ref[0:1, :] = h


def _inproj_body(x_ref, g_ref, cos_ref, sin_ref, wqk_ref, wv_ref, wu_ref, wrkv_ref, wlru_ref, wlora_ref, *rest):
    n_dil = len(DILATIONS) - 1
    lru_consts, rest = rest[:7], rest[7:]
    perm_refs, (h_ref, u_ref, rkv_ref, lru_ref, lora_ref, qkv_ref), rest = rest[:n_dil], rest[n_dil:n_dil + 6], rest[n_dil + 6:]
    dil_refs, lru_scratch = rest[:n_dil], rest[n_dil:]
    tm = x_ref.shape[0]

    @pl.when(pl.program_id(1) == 0)
    def _():
        for ref in lru_scratch[1:]:
            ref[...] = jnp.zeros_like(ref)

    hb = _rms(x_ref[...], g_ref[...]).astype(BF16)
    h_ref[...] = hb
    _lru_apply(_dot(hb, wlru_ref[...]), *lru_consts, lru_ref, *lru_scratch)
    qk = _dot(hb, wqk_ref[...])
    reps = 2 * WIDTH // LANES
    cos = jnp.concatenate([cos_ref[...]] * reps, axis=-1)
    sin = jnp.concatenate([sin_ref[...]] * reps, axis=-1)
    lane = lax.broadcasted_iota(jnp.int32, qk.shape, 1)
    first_half = (lane % HEAD_DIM) < (HEAD_DIM // 2)
    partner = jnp.where(first_half,
                        pltpu.roll(qk, 2 * WIDTH - HEAD_DIM // 2, 1),
                        pltpu.roll(qk, HEAD_DIM // 2, 1))
    rot = qk * cos + partner * sin
    qkv = jnp.concatenate([(rot[:, :WIDTH] * Q_SCALE).astype(BF16), rot[:, WIDTH:].astype(BF16),
                           _dot(hb, wv_ref[...]).astype(BF16)], axis=-1)
    qkv_ref[...] = qkv
    for dil, perm_ref, out_ref in zip(DILATIONS[1:], perm_refs, dil_refs):
        grouped = _dot(perm_ref[...], qkv).astype(BF16)
        rows = tm // dil
        for r in range(dil):
            out_ref[:, r * 3 * WIDTH:(r + 1) * 3 * WIDTH] = grouped[r * rows:(r + 1) * rows]
    u_ref[...] = _dot(hb, wu_ref[...])
    rkv_ref[...] = _dot(hb, wrkv_ref[...])
    lora_ref[...] = _dot(hb, wlora_ref[...])


def _inproj(x, g, cos, sin, wqk, wv, wu, wrkv, wlru, wlora, lru_consts, tm=256):
    b, s, d = x.shape
    grid = (b, s // tm)
    row = lambda w: pl.BlockSpec((None, tm, w), lambda bi, i: (bi, i, 0))
    tab = pl.BlockSpec((tm, LANES), lambda bi, i: (i, 0))
    widths = (d, WIDTH, 3 * WIDTH, WIDTH, 2 * LORA_HALF, 3 * WIDTH)
    dtypes = (BF16, F32, F32, F32, F32, BF16)
    out_specs = [row(w) for w in widths]
    out_shape = [jax.ShapeDtypeStruct((b, s, w), dt) for w, dt in zip(widths, dtypes)]
    out_specs[1] = _seg_view_spec(s, tm)
    out_shape[1] = jax.ShapeDtypeStruct((b, s // S5_SEGS, S5_SEGS * WIDTH), F32)
    for dil in DILATIONS[1:]:
        out_specs.append(pl.BlockSpec((None, tm // dil, dil * 3 * WIDTH), lambda bi, i: (bi, i, 0)))
        out_shape.append(jax.ShapeDtypeStruct((b, s // dil, dil * 3 * WIDTH), BF16))
    consts = (wqk, wv, wu, wrkv, wlru, wlora) + tuple(lru_consts) + tuple(_dilated_perm(tm, dil) for dil in DILATIONS[1:])
    return pl.pallas_call(
        _inproj_body,
        grid=grid,
        in_specs=[row(d), _const_spec((1, d)), tab, tab] + [_const_spec(c.shape) for c in consts],
        out_specs=out_specs,
        out_shape=out_shape,
        scratch_shapes=[pltpu.VMEM((tm, WIDTH), F32), pltpu.VMEM((SUBLANES, WIDTH), F32), pltpu.VMEM((SUBLANES, WIDTH), F32)],
        compiler_params=_params(("parallel", "arbitrary")),
        name="inproj",
    )(x, g, cos, sin, *consts)


def _attn_body(q_ref, kc_ref, kp_ref, vc_ref, o_ref, l_ref, vt_ref):
    i = pl.program_id(2)
    blk = ATT_BLOCK
    n_sub = q_ref.shape[0] // blk

    @pl.when(i == 0)
    def _():
        vt_ref[...] = jnp.zeros_like(vt_ref)

    rows = [slice(j * blk, (j + 1) * blk) for j in range(n_sub)]
    vts = [vt_ref[...]] + [vc_ref[r, :].astype(F32).T.astype(BF16) for r in rows]
    vt_ref[...] = vts[-1]
    key = lax.broadcasted_iota(jnp.int32, (2 * blk, blk), 0)
    qry = lax.broadcasted_iota(jnp.int32, (2 * blk, blk), 1)
    in_cur = jnp.logical_and(key >= blk, key - blk <= qry)
    in_prev = jnp.logical_and(key < blk, key >= qry)
    masks = [jnp.logical_or(in_cur, jnp.logical_and(in_prev, i > 0))] + [jnp.logical_or(in_cur, in_prev)] * (n_sub - 1)
    heads = [slice(h * HEAD_DIM, (h + 1) * HEAD_DIM) for h in range(N_HEADS)]
    pairs = [(j, sl) for j in range(n_sub) for sl in heads]
    prev_keys = lambda j, sl: kp_ref[:, sl] if j == 0 else kc_ref[rows[j - 1], sl]
    keys = [jnp.concatenate([prev_keys(j, sl), kc_ref[rows[j], sl]], axis=0) for j, sl in pairs]
    st = [jnp.where(masks[j], _dot_nt(k, q_ref[rows[j], sl]), NEG_BIG) for k, (j, sl) in zip(keys, pairs)]
    ms = [jnp.max(s, axis=0, keepdims=True) for s in st]
    ps = [jnp.exp2(s - m).astype(BF16) for s, m in zip(st, ms)]
    ones = jnp.ones((2 * SUBLANES, 2 * blk), BF16)
    outs, dens = [], []
    for n, (j, sl) in enumerate(pairs):
        vt = jnp.concatenate([vts[j][sl, :], vts[j + 1][sl, :]], axis=1)
        res = _dot(jnp.concatenate([vt, ones], axis=0), ps[n])
        dens.append(res[HEAD_DIM:HEAD_DIM + 1, :])
        outs.append(res[:HEAD_DIM, :] * (1.0 / dens[n]))
    pad = [jnp.zeros((LANES - N_HEADS, blk), F32)]
    for j in range(n_sub):
        mine = slice(j * N_HEADS, (j + 1) * N_HEADS)
        o_ref[rows[j], :] = jnp.concatenate(outs[mine], axis=0).T.astype(BF16)
        lse = jnp.concatenate([(m + jnp.log2(d)) * math.log(2.0) for m, d in zip(ms[mine], dens[mine])] + pad, axis=0)
        l_ref[rows[j], :] = lse.T


def _attention(qkv_view, dil):
    b, sub, _ = qkv_view.shape
    w = WIDTH
    n_blocks = sub // ATT_BLOCK
    n_sub = max(k for k in (8, 4, 2, 1) if n_blocks % k == 0)
    step = n_sub * ATT_BLOCK
    cur = lambda j: pl.BlockSpec((None, step, w), lambda bi, r, i: (bi, i, 3 * r + j))
    prev_k = pl.BlockSpec((None, ATT_BLOCK, w), lambda bi, r, i: (bi, jnp.maximum(i * n_sub - 1, 0), 3 * r + 1))
    return pl.pallas_call(
        _attn_body,
        grid=(b, dil, n_blocks // n_sub),
        in_specs=[cur(0), cur(1), prev_k, cur(2)],
        out_specs=[pl.BlockSpec((None, step, w), lambda bi, r, i: (bi, i, r)),
                   pl.BlockSpec((None, step, LANES), lambda bi, r, i: (bi, i, r))],
        out_shape=[jax.ShapeDtypeStruct((b, sub, dil * w), BF16),
                   jax.ShapeDtypeStruct((b, sub, dil * LANES), F32)],
        scratch_shapes=[pltpu.VMEM((w, ATT_BLOCK), BF16)],
        compiler_params=_params(("parallel", "parallel", "arbitrary")),
        name=f"attn_d{dil}",
    )(*([qkv_view] * 4))


def _gelu_tanh(x):
    return 0.5 * x * (1.0 + jnp.tanh(math.sqrt(2.0 / math.pi) * (x + 0.044715 * (x * x * x))))


def _s5_slab(u_ref, perm_ref):
    u_stack = jnp.concatenate([u_ref[:, sg * WIDTH:(sg + 1) * WIDTH] for sg in range(S5_SEGS)], axis=0)
    u_slab = _dot(perm_ref[...], u_stack.astype(BF16)).astype(BF16)
    return u_stack, u_slab


def _s5_scan(u_slab, bbd_ref, asame_ref, across_ref, bu_ref, st_ref, store):
    hs = S5_HALF_STATES
    cin = WIDTH // S5_HALVES
    cols = [slice(half * 2 * hs, (half + 1) * 2 * hs) for half in range(S5_HALVES)]
    for half in range(S5_HALVES):
        bu_ref[half] = _dot(u_slab[:, half * cin:(half + 1) * cin], bbd_ref[half])
    xs = [st_ref[:, c] for c in cols]
    for t in range(bu_ref.shape[1] // S5_SEGS):
        rows = slice(t * S5_SEGS, (t + 1) * S5_SEGS)
        for half, c in enumerate(cols):
            x = xs[half]
            swapped = jnp.concatenate([x[:, hs:], x[:, :hs]], axis=-1)
            x = asame_ref[:, c] * x + across_ref[:, c] * swapped + bu_ref[half, rows, :]
            if store:
                bu_ref[half, rows, :] = x
            xs[half] = x
    for half, c in enumerate(cols):
        st_ref[:, c] = xs[half]


def _s5_state_body(u_ref, perm_ref, bbd_ref, asame_ref, across_ref, end_ref, bu_ref, st_ref):
    @pl.when(pl.program_id(1) == 0)
    def _():
        st_ref[...] = jnp.zeros_like(st_ref)

    _, u_slab = _s5_slab(u_ref, perm_ref)
    _s5_scan(u_slab, bbd_ref, asame_ref, across_ref, bu_ref, st_ref, store=False)

    @pl.when(pl.program_id(1) == pl.num_programs(1) - 1)
    def _():
        end_ref[...] = st_ref[...]


def _cmul(z, w):
    hs = z.shape[-1] // 2
    zr, zi, wr, wi = z[:, :hs], z[:, hs:], w[:, :hs], w[:, hs:]
    return jnp.concatenate([zr * wr - zi * wi, zr * wi + zi * wr], axis=-1)


def _s5_out_body(seg_len, u_ref, end_ref, perm_ref, permt_ref, bbd_ref, asame_ref, across_ref, aplain_ref,
                 cbd_ref, d_ref, wglu_ref, bglu_ref, o_ref, bu_ref, st_ref):
    hs = S5_HALF_STATES
    tt = u_ref.shape[0]

    @pl.when(pl.program_id(1) == 0)
    def _():
        for half in range(S5_HALVES):
            cols = slice(half * 2 * hs, (half + 1) * 2 * hs)
            base = aplain_ref[:, cols]
            power = None
            e = seg_len
            while e:
                if e & 1:
                    power = base if power is None else _cmul(power, base)
                e >>= 1
                if e:
                    base = _cmul(base, base)
            x = jnp.zeros((1, 2 * hs), F32)
            for sg in range(S5_SEGS):
                st_ref[sg:sg + 1, cols] = x
                x = _cmul(x, power) + end_ref[sg:sg + 1, cols]

    u_stack, u_slab = _s5_slab(u_ref, perm_ref)
    _s5_scan(u_slab, bbd_ref, asame_ref, across_ref, bu_ref, st_ref, store=True)
    ys = [_dot(bu_ref[half].astype(BF16), cbd_ref[half]) for half in range(S5_HALVES)]
    hi, lo = _split2(jnp.concatenate(ys, axis=-1))
    y = _dot(permt_ref[...], hi) + _dot(permt_ref[...], lo) + d_ref[...] * u_stack
    z = _dot(_gelu_tanh(y).astype(BF16), wglu_ref[...]) + bglu_ref[...]
    out = z[:, :WIDTH] * _sigmoid(z[:, WIDTH:])
    for sg in range(S5_SEGS):
        o_ref[:, sg * WIDTH:(sg + 1) * WIDTH] = out[sg * tt:(sg + 1) * tt]


def _s5(u_view, bbd, a_same, a_cross, a_plain, cbd, d_skip, wglu, bglu, tt=64):
    b, seg_len, wide = u_view.shape
    m = tt * S5_SEGS
    dst = np.arange(m)
    src = (dst % S5_SEGS) * tt + dst // S5_SEGS
    perm = jnp.asarray(src[:, None] == np.arange(m)[None, :], BF16)
    row = pl.BlockSpec((None, tt, wide), lambda bi, i: (bi, i, 0))
    ends = pl.BlockSpec((None, S5_SEGS, 2 * S5_NSTATE), lambda bi, i: (bi, 0, 0))
    scratch = [pltpu.VMEM((S5_HALVES, m, 2 * S5_HALF_STATES), F32), pltpu.VMEM((S5_SEGS, 2 * S5_NSTATE), F32)]
    consts1 = (perm, bbd, a_same, a_cross)
    seg_end = pl.pallas_call(
        _s5_state_body,
        grid=(b, seg_len // tt),
        in_specs=[row] + [_const_spec(c.shape) for c in consts1],
        out_specs=ends,
        out_shape=jax.ShapeDtypeStruct((b, S5_SEGS, 2 * S5_NSTATE), F32),
        scratch_shapes=scratch,
        compiler_params=_params(("parallel", "arbitrary")),
        name="s5_state",
    )(u_view, *consts1)
    consts2 = (perm, perm.T, bbd, a_same, a_cross, a_plain, cbd, d_skip, wglu, bglu)
    return pl.pallas_call(
        functools.partial(_s5_out_body, seg_len),
        grid=(b, seg_len // tt),
        in_specs=[row, ends] + [_const_spec(c.shape) for c in consts2],
        out_specs=row,
        out_shape=jax.ShapeDtypeStruct((b, seg_len, wide), F32),
        scratch_shapes=scratch,
        compiler_params=_params(("parallel", "arbitrary")),
        name="s5_out",
    )(u_view, seg_end, *consts2)


V_MU_R, V_MU_K, V_MU_V, V_W0, V_A0, V_V0, V_KK, V_KA, V_RK, V_GNW, V_GNB = range(11)


def _rwkv_body(has_vres, *refs):
    if has_vres:
        (rkv_ref, lora_ref, vf_ref, vec_ref, w2_ref, a2_ref, g2_ref, v2_ref, hsum_ref,
         y_ref, st_ref, carry_ref, ysc_ref) = refs
    else:
        (rkv_ref, lora_ref, vec_ref, w2_ref, a2_ref, g2_ref, hsum_ref,
         y_ref, vout_ref, st_ref, carry_ref, ysc_ref) = refs
    L = RWKV_CHUNK
    nb, nj = rkv_ref.shape[0], rkv_ref.shape[1] // L
    m = nb * nj * L
    w = WIDTH

    @pl.when(pl.program_id(0) == 0)
    def _():
        st_ref[...] = jnp.zeros_like(st_ref)
        carry_ref[...] = jnp.zeros_like(carry_ref)

    vec = vec_ref[...]
    row = lambda j: vec[j:j + 1, :]
    hsum = hsum_ref[...]

    rkv = rkv_ref[...].reshape(m, 3 * w)
    lora = lora_ref[...].reshape(m, 2 * LORA_HALF)
    cat = jnp.concatenate([rkv, lora[:, LORA_HALF:]], axis=-1)
    row_id = lax.broadcasted_iota(jnp.int32, (m, 1), 0)
    prev = pltpu.roll(cat, 1, 0)
    for b in range(nb):
        prev = jnp.where(row_id == b * nj * L, carry_ref[b], prev)
        carry_ref[b] = cat[(b + 1) * nj * L - 1:(b + 1) * nj * L, :]
    lerp = lambda j, mu: rkv[:, j * w:(j + 1) * w] + (prev[:, j * w:(j + 1) * w] - rkv[:, j * w:(j + 1) * w]) * mu
    r = lerp(0, row(V_MU_R))
    k = lerp(1, row(V_MU_K))
    v = lerp(2, row(V_MU_V))
    pre = lora[:, :LORA_HALF] + prev[:, 3 * w:]

    wx = row(V_W0) + _dot(jnp.tanh(pre[:, 0:64]).astype(BF16), w2_ref[...])
    logw = -jnp.exp(-_softplus(-wx) - 0.5)
    a = _sigmoid(row(V_A0) + _dot(pre[:, 64:128].astype(BF16), a2_ref[...]))
    g = _dot(_sigmoid(pre[:, 128:256]).astype(BF16), g2_ref[...])
    if has_vres:
        vf = vf_ref[...].reshape(m, w)
        v = v + (vf - v) * _sigmoid(row(V_V0) + _dot(pre[:, 256:320].astype(BF16), v2_ref[...]))
    else:
        vout_ref[...] = v.reshape(nb, nj * L, w)

    kk = k * row(V_KK)
    k = k * (1.0 + (a - 1.0) * row(V_KA))
    kk_sq, rk_sum = _head_sums([kk * kk, r * k * row(V_RK)], hsum)
    kk = kk * lax.rsqrt(jnp.maximum(kk_sq, 1e-24))
    bonus = rk_sum * v

    pair = 2 * L
    mi = lax.broadcasted_iota(jnp.int32, (pair, pair), 0)
    mj = lax.broadcasted_iota(jnp.int32, (pair, pair), 1)
    same_chunk = (mi // L) == (mj // L)
    tri = jnp.where(jnp.logical_and(same_chunk, mj <= mi), 1.0, 0.0).astype(BF16)
    parts = _split3(logw)
    cl = jnp.concatenate([sum(_dot(tri, part[lo:lo + pair]) for part in parts) for lo in range(0, m, pair)], axis=0)
    e_neg = jnp.exp(-cl)
    a_t = ((-kk) * jnp.exp(cl - logw)).astype(BF16)
    b_t = ((kk * a) * e_neg).astype(BF16)
    k_t = (k * e_neg).astype(BF16)
    r_t = (r * jnp.exp(cl)).astype(BF16)
    v_b = v.astype(BF16)

    pw_lanes = 2 * HEAD_DIM
    lane = lax.broadcasted_iota(jnp.int32, (L, pw_lanes), 1)
    step = lax.broadcasted_iota(jnp.int32, (L, pw_lanes), 0)
    head0 = lane < HEAD_DIM
    upper_strict = step < lane % HEAD_DIM
    lower_incl = lane % HEAD_DIM <= step
    eye = (step == lane % HEAD_DIM).astype(F32)

    def per_head_rows(x):
        zero = jnp.zeros_like(x)
        return jnp.concatenate([jnp.where(head0, x, zero), jnp.where(head0, zero, x)], axis=0)

    chains = []
    for b in range(nb):
        for j in range(nj):
            lo = (b * nj + j) * L
            rows = slice(lo, lo + L)
            v_tt = v[rows].T.astype(BF16)
            p_end = jnp.exp(cl[lo + L - 1:lo + L, :])
            for p in range(N_HEADS // 2):
                sl = slice(p * pw_lanes, (p + 1) * pw_lanes)
                vtt = jnp.concatenate([v_tt[p * pw_lanes:p * pw_lanes + HEAD_DIM, :],
                                       v_tt[p * pw_lanes + HEAD_DIM:(p + 1) * pw_lanes, :]], axis=1)
                chains.append(dict(b=b, j=j, p=p, rows=rows, sl=sl, a_bd=per_head_rows(a_t[rows, sl]),
                                   b_bd=per_head_rows(b_t[rows, sl]), k_bd=per_head_rows(k_t[rows, sl]),
                                   bk=jnp.concatenate([b_t[rows, sl], k_t[rows, sl]], axis=0),
                                   r=r_t[rows, sl], v_bd=per_head_rows(v_b[rows, sl]), vtt=vtt, p_end=p_end[:, sl]))
    for c in chains:
        both = _dot_nt(c["bk"], c["a_bd"])
        c["tab"] = jnp.where(upper_strict, both[:L], 0.0)
        c["tak_bd"] = per_head_rows(jnp.where(upper_strict, both[L:], 0.0).astype(BF16))
        rbk = _dot_nt(c["r"], jnp.concatenate([c["b_bd"], c["k_bd"]], axis=0))
        c["trb"] = jnp.where(lower_incl, rbk[:, :pw_lanes], 0.0).astype(BF16)
        c["trk"] = jnp.where(lower_incl, rbk[:, pw_lanes:], 0.0).astype(BF16)
    t_idx = lane % HEAD_DIM
    coupling = lambda sz: jnp.logical_and(jnp.logical_and(step // (2 * sz) == t_idx // (2 * sz),
                                                         step // sz != t_idx // sz), step < t_idx)
    for c in chains:
        c["inv"] = eye + jnp.where(coupling(1), c["tab"], 0.0)
    sz = 2
    while sz < L:
        mask = coupling(sz)
        for c in chains:
            inv_bd = per_head_rows(c["inv"].astype(BF16))
            off_bd = per_head_rows(jnp.where(mask, c["tab"], 0.0).astype(BF16))
            c["inv"] = c["inv"] + _dot(_dot(c["inv"].astype(BF16), off_bd).astype(BF16), inv_bd)
        sz *= 2
    for c in chains:
        c["inv_bd"] = per_head_rows(c["inv"].astype(BF16))
    state = {(b, p): st_ref[b, p] for b in range(nb) for p in range(N_HEADS // 2)}
    for j in range(nj):
        group = [c for c in chains if c["j"] == j]
        for c in group:
            c["st"] = state[c["b"], c["p"]]
            c["stb"] = c["st"].astype(BF16)
            c["rhs"] = _dot_nt(c["stb"], c["a_bd"]) + _dot(c["vtt"], c["tak_bd"])
        for c in group:
            c["ut"] = _dot(c["rhs"].astype(BF16), c["inv_bd"]).astype(BF16)
        for c in group:
            state[c["b"], c["p"]] = (c["st"] + _dot(c["ut"], c["b_bd"]) + _dot(c["vtt"], c["k_bd"])) * c["p_end"]
        for c in group:
            lhs = jnp.concatenate([c["r"], c["trb"]], axis=1)
            rhs = jnp.concatenate([per_head_rows(c["stb"]), per_head_rows(c["ut"])], axis=1)
            ysc_ref[c["rows"], c["sl"]] = _dot_nt(lhs, rhs) + _dot(c["trk"], c["v_bd"])
    for (b, p), value in state.items():
        st_ref[b, p] = value

    y = ysc_ref[...]
    mean = _head_sums([y], hsum)[0] * (1.0 / HEAD_DIM)
    dev = y - mean
    var = _head_sums([dev * dev], hsum)[0] * (1.0 / HEAD_DIM)
    y = dev * lax.rsqrt(var + RWKV_GN_EPS) * row(V_GNW) + row(V_GNB)
    y_ref[...] = ((y + bonus) * g).reshape(nb, nj * L, w)


def _rwkv(rkv, lora, v_first, vec, w2, a2, g2, v2, hsum):
    b, s, _ = rkv.shape
    n_chunks = s // RWKV_CHUNK
    L = RWKV_CHUNK * max(k for k in (4, 2, 1) if n_chunks % k == 0)
    has_vres = v_first is not None
    row = lambda wd: pl.BlockSpec((b, L, wd), lambda i: (0, i, 0))
    ins = [rkv, lora] + ([v_first] if has_vres else []) + [vec, w2, a2, g2] + ([v2] if has_vres else []) + [hsum]
    in_specs = [row(3 * WIDTH), row(2 * LORA_HALF)] + ([row(WIDTH)] if has_vres else [])
    in_specs += [_const_spec(c.shape) for c in ins[len(in_specs):]]
    y_shape = jax.ShapeDtypeStruct((b, s, WIDTH), F32)
    out = pl.pallas_call(
        functools.partial(_rwkv_body, has_vres),
        grid=(s // L,),
        in_specs=in_specs,
        out_specs=row(WIDTH) if has_vres else [row(WIDTH), row(WIDTH)],
        out_shape=y_shape if has_vres else [y_shape, y_shape],
        scratch_shapes=[pltpu.VMEM((b, N_HEADS // 2, HEAD_DIM, 2 * HEAD_DIM), F32),
                        pltpu.VMEM((b, 1, 3 * WIDTH + LORA_HALF), F32),
                        pltpu.VMEM((b * L, WIDTH), F32)],
        compiler_params=_params(("arbitrary",)),
        name="rwkv_l1" if has_vres else "rwkv_l0",
    )(*ins)
    return (out, v_first) if has_vres else (out[0], out[1])


def _from_residue_view(ref, dil, unperm):
    if dil == 1:
        return ref[...].astype(F32)
    width = ref.shape[1] // dil
    stacked = jnp.concatenate([ref[:, r * width:(r + 1) * width] for r in range(dil)], axis=0)
    if stacked.dtype == BF16:
        return _dot(unperm, stacked)
    hi, lo = _split2(stacked)
    return _dot(unperm, hi) + _dot(unperm, lo)


def _merge_ffn_body(final, x_ref, h_ref, o1_ref, o2_ref, o3_ref, l1_ref, l2_ref, l3_ref, yb_ref, yc_ref, yd_ref,
                    wg_ref, wb_ref, wo_ref, exp_ref, up2_ref, up3_ref,
                    gn_ref, fg_ref, fu_ref, fd_ref, gf_ref, out_ref):
    hb = h_ref[...]
    unperms = (None, up2_ref[...], up3_ref[...])
    o1, o2, o3 = (_from_residue_view(r, d, p) for r, d, p in zip((o1_ref, o2_ref, o3_ref), DILATIONS, unperms))
    l1, l2, l3 = (_from_residue_view(r, d, p) for r, d, p in zip((l1_ref, l2_ref, l3_ref), DILATIONS, unperms))
    m = jnp.maximum(jnp.maximum(l1, l2), l3)
    e1, e2, e3 = jnp.exp(l1 - m), jnp.exp(l2 - m), jnp.exp(l3 - m)
    inv = 1.0 / (e1 + e2 + e3)
    expand = exp_ref[...]
    ya = (_dot_sel(e1 * inv, expand) * o1 + _dot_sel(e2 * inv, expand) * o2 + _dot_sel(e3 * inv, expand) * o3)
    merged = None
    for n, y in enumerate((ya, yb_ref[...], yc_ref[...], yd_ref[...])):
        gate = _sigmoid(_dot(hb, wg_ref[:, n * D_MODEL:(n + 1) * D_MODEL]))
        term = gate * _dot(y.astype(BF16), wb_ref[n])
        merged = term if merged is None else merged + term
    x = x_ref[...] + _dot(merged.astype(BF16), wo_ref[...])
    h2 = _rms(x, gn_ref[...]).astype(BF16)
    gate = _dot(h2, fg_ref[...])
    up = _dot(h2, fu_ref[...])
    y = x + _dot((gate * _sigmoid(gate) * up).astype(BF16), fd_ref[...])
    out_ref[...] = _rms(y, gf_ref[...]) if final else y


def _merge_ffn(x, h, os, ls, yb, yc, yd, wg, wb, wo, expand, gn, fg, fu, fd, gf, final, tm=256):
    b, s, d = x.shape
    row = lambda wd: pl.BlockSpec((None, tm, wd), lambda bi, i: (bi, i, 0))
    res = lambda wd: [pl.BlockSpec((None, tm // dil, dil * wd), lambda bi, i: (bi, i, 0)) for dil in DILATIONS]
    consts = (wg, wb, wo, expand) + tuple(_dilated_perm(tm, dil).T for dil in DILATIONS[1:]) + (gn, fg, fu, fd, gf)
    return pl.pallas_call(
        functools.partial(_merge_ffn_body, final),
        grid=(b, s // tm),
        in_specs=[row(d), row(d)] + res(WIDTH) + res(LANES)
        + [_seg_view_spec(s, tm), row(WIDTH), row(WIDTH)] + [_const_spec(c.shape) for c in consts],
        out_specs=row(d),
        out_shape=jax.ShapeDtypeStruct((b, s, d), F32),
        compiler_params=_params(("parallel", "parallel")),
        name="merge_ffn_final" if final else "merge_ffn",
    )(x, h, *os, *ls, yb, yc, yd, *consts)


def _rope_tables(s):
    inv = 1.0 / (ROPE_THETA ** (jnp.arange(0, HEAD_DIM, 2, dtype=F32) / HEAD_DIM))
    ang = jnp.arange(s, dtype=F32)[:, None] * inv[None, :]
    cos, sin = jnp.cos(ang), jnp.sin(ang)
    reps = LANES // HEAD_DIM
    cos_t = jnp.tile(jnp.concatenate([cos, cos], axis=-1), (1, reps))
    sin_t = jnp.tile(jnp.concatenate([-sin, sin], axis=-1), (1, reps))
    return cos_t, sin_t


def _block_diag(blocks):
    g, r, c = blocks.shape
    eye = jnp.eye(g, dtype=blocks.dtype)
    return (eye[:, None, :, None] * blocks[:, :, None, :]).reshape(g * r, g * c)


def _s5_params(lam_re, lam_im, log_dt, b_re, b_im, c_re, c_im):
    lr, li = lam_re.astype(F32), lam_im.astype(F32)
    dt = jnp.exp(log_dt.astype(F32))[:, None]
    mag = jnp.exp(lr * dt)
    ab_re, ab_im = mag * jnp.cos(li * dt), mag * jnp.sin(li * dt)
    nr = ab_re - 1.0
    den = lr * lr + li * li
    f_re = (nr * lr + ab_im * li) / den
    f_im = (ab_im * lr - nr * li) / den
    bb_re = f_re[..., None] * b_re - f_im[..., None] * b_im
    bb_im = f_re[..., None] * b_im + f_im[..., None] * b_re
    gh = S5_GROUPS // S5_HALVES
    swap = lambda t: jnp.transpose(t, (0, 2, 1))
    halves = lambda t: [t[h * gh:(h + 1) * gh] for h in range(S5_HALVES)]
    bbd = jnp.stack([jnp.concatenate([_block_diag(swap(re)), _block_diag(swap(im))], axis=-1)
                     for re, im in zip(halves(bb_re), halves(bb_im))]).astype(BF16)
    cbd = jnp.stack([jnp.concatenate([_block_diag(swap(re)), -_block_diag(swap(im))], axis=0)
                     for re, im in zip(halves(c_re), halves(c_im))]).astype(BF16)
    flat = lambda t: [h.reshape(1, -1) for h in halves(t)]
    cat = lambda parts: jnp.concatenate(parts, axis=-1)
    a_plain = cat([cat([re, im]) for re, im in zip(flat(ab_re), flat(ab_im))])
    a_same = cat([cat([re, re]) for re in flat(ab_re)])
    a_cross = cat([cat([-im, im]) for im in flat(ab_im)])
    rep = lambda t: jnp.broadcast_to(t, (S5_SEGS, t.shape[-1]))
    return bbd, rep(a_same), rep(a_cross), a_plain, cbd


def _lora_weights(mu_wag, w1, a1, g1, mu_v, v1):
    d = w1.shape[0]
    pad = lambda t: jnp.pad(t, ((0, 0), (0, LORA_HALF - t.shape[1])))
    v1 = jnp.zeros((d, 32), F32) if v1 is None else v1
    mu_v = jnp.zeros((d,), F32) if mu_v is None else mu_v
    mats = (w1, a1, g1, v1)
    mus = (mu_wag[0], mu_wag[1], mu_wag[2], mu_v)
    keep = pad(jnp.concatenate([m * (1.0 - mu)[:, None] for m, mu in zip(mats, mus)], axis=1))
    shifted = pad(jnp.concatenate([m * mu[:, None] for m, mu in zip(mats, mus)], axis=1))
    return jnp.concatenate([keep, shifted], axis=1).astype(BF16)


def kernel(x, norm_mix, w_in, s5_lam_re, s5_lam_im, s5_log_dt, s5_b_re, s5_b_im, s5_c_re, s5_c_im, s5_d, s5_w_glu, s5_b_glu, rw_mu_rkv, rw_mu_wag, rw_w0, rw_w1, rw_w2, rw_a0, rw_a1, rw_a2, rw_g1, rw_g2, rw_k_k, rw_k_a, rw_r_k, rw_gn_w, rw_gn_b, rw_mu_v, rw_v0, rw_v1, rw_v2, lru_conv_w, lru_conv_b, lru_w_a, lru_b_a, lru_w_x, lru_b_x, lru_lam, w_branch, w_out, norm_ffn, w_ffn_gate, w_ffn_up, w_ffn_down, norm_final):
    b, s, d = x.shape
    depth = w_in.shape[0]
    w = WIDTH
    cos_t, sin_t = _rope_tables(s)
    head_id = np.arange(w) // HEAD_DIM
    hsum = jnp.asarray(head_id[:, None] == head_id[None, :], BF16)
    expand = jnp.asarray(np.arange(LANES)[:, None] == head_id[None, :], BF16)
    bf = lambda t: t.astype(BF16)
    row = lambda t: t.reshape(1, -1).astype(F32)

    v_first = None
    for l in range(depth):
        wl = w_in[l]
        cols = np.cumsum((0, w, w, w, w, w, w, w, w))
        seg = lambda i, j: bf(wl[:, cols[i]:cols[j]])
        has_vres = l > 0
        wlora = _lora_weights(rw_mu_wag[l], rw_w1[l], rw_a1[l], rw_g1[l],
                              rw_mu_v[l - 1] if has_vres else None, rw_v1[l - 1] if has_vres else None)
        gate_w = lambda t: bf(jnp.stack([_block_diag(t[:LRU_BLOCKS // 2]), _block_diag(t[LRU_BLOCKS // 2:])]))
        lru_consts = (lru_conv_w[l].astype(F32), row(lru_conv_b[l]), gate_w(lru_w_a[l]), row(lru_b_a[l]),
                      gate_w(lru_w_x[l]), row(lru_b_x[l]), row(lru_lam[l]))
        h, u, rkv, y_d, lora, *qkv_views = _inproj(
            x, row(norm_mix[l]), cos_t, sin_t, seg(0, 2), seg(2, 3), seg(3, 4), seg(4, 7), seg(7, 8), wlora, lru_consts)

        att = [_attention(view, dil) for view, dil in zip(qkv_views, DILATIONS)]

        s5p = _s5_params(s5_lam_re[l], s5_lam_im[l], s5_log_dt[l], s5_b_re[l], s5_b_im[l], s5_c_re[l], s5_c_im[l])
        y_b = _s5(u, *s5p, row(s5_d[l]), bf(s5_w_glu[l]), row(s5_b_glu[l]))

        zero = jnp.zeros((w,), F32)
        vec = jnp.stack([rw_mu_rkv[l, 0], rw_mu_rkv[l, 1], rw_mu_rkv[l, 2], rw_w0[l], rw_a0[l],
                         rw_v0[l - 1] if has_vres else zero, rw_k_k[l], rw_k_a[l], rw_r_k[l].reshape(-1),
                         rw_gn_w[l], rw_gn_b[l]] + [zero] * 5).astype(F32)
        v2 = bf(jnp.pad(rw_v2[l - 1], ((0, 32), (0, 0)))) if has_vres else None
        y_c, v_c = _rwkv(rkv, lora, v_first, vec, bf(rw_w2[l]), bf(rw_a2[l]), bf(rw_g2[l]), v2, hsum)
        if l == 0:
            v_first = v_c

        x = _merge_ffn(x, h, [o for o, _ in att], [ls for _, ls in att], y_b, y_c, y_d,
                       bf(wl[:, cols[8]:]), bf(w_branch[l]), bf(w_out[l]), expand,
                       row(norm_ffn[l]), bf(w_ffn_gate[l]), bf(w_ffn_up[l]), bf(w_ffn_down[l]), row(norm_final),
                       final=(l == depth - 1))
    return x
```
